```python
import math
import jax, jax.numpy as jnp
from jax import lax
import numpy as np

D_MODEL = 1024
BATCH = 8
SEQ = 4096
DEPTH = 4

HEAD_DIM = 64
EPS = 1e-6

SWA_Q_HEADS = 8
SWA_KV_HEADS = 2
SWA_GROUP = SWA_Q_HEADS // SWA_KV_HEADS
WINDOW = 128
BLOCK = 128
SWA_WIDTH = SWA_Q_HEADS * HEAD_DIM
KV_WIDTH = SWA_KV_HEADS * HEAD_DIM

SSM_WIDTH = D_MODEL // 2
SSM_GROUP_CH = 16
SSM_GROUPS = SSM_WIDTH // SSM_GROUP_CH
SSM_STATE = 64
DT_MIN = 0.001
DT_MAX = 0.1

EVEN_IN = SWA_WIDTH + 2 * KV_WIDTH + SSM_WIDTH
EVEN_MIX = SWA_WIDTH + SSM_WIDTH

SB_HEADS = D_MODEL // HEAD_DIM
SB_WIDTH = SB_HEADS * HEAD_DIM
SB_BLOCK = 128

D_FF = -(-8 * D_MODEL // (3 * 256)) * 256

N_EVEN = (DEPTH + 1) // 2
N_ODD = DEPTH // 2

kernel_name = "hybrid_swa_s5_stickbreaking_trunk"


def rmsnorm(x, g):
    xf = x.astype(jnp.float32)
    y = xf * lax.rsqrt(jnp.mean(xf * xf, axis=-1, keepdims=True) + EPS)
    return (y * g.astype(jnp.float32)).astype(x.dtype)


def swa_sink_attention(q, k, v, sinks):
    b, l, _, d = q.shape
    nb = l // BLOCK
    qb = q.reshape(b, nb, BLOCK, SWA_KV_HEADS, SWA_GROUP, d)

    def with_prev(t):
        t = t.reshape(b, nb, BLOCK, SWA_KV_HEADS, d)
        prev = jnp.concatenate([jnp.zeros_like(t[:, :1]), t[:, :-1]], axis=1)
        return jnp.concatenate([prev, t], axis=2)

    kb, vb = with_prev(k), with_prev(v)
    scores = jnp.einsum('bnqhgd,bnkhd->bnhgqk', qb, kb).astype(jnp.float32) * (d ** -0.5)
    qpos = jnp.arange(BLOCK) + BLOCK
    kpos = jnp.arange(2 * BLOCK)
    diff = qpos[:, None] - kpos[None, :]
    band = (diff >= 0) & (diff < WINDOW)
    has_prev = (jnp.arange(nb)[:, None] > 0) | (kpos[None, :] >= BLOCK)
    mask = (band[None, :, :] & has_prev[:, None, :])[None, :, None, None]
    scores = jnp.where(mask, scores, -jnp.inf)
    sink = sinks.astype(jnp.float32).reshape(SWA_KV_HEADS, SWA_GROUP)[None, None, :, :, None, None]
    m = jnp.maximum(scores.max(axis=-1, keepdims=True), sink)
    p = jnp.exp(scores - m)
    probs = p / (p.sum(axis=-1, keepdims=True) + jnp.exp(sink - m))
    o = jnp.einsum('bnhgqk,bnkhd->bnqhgd', probs.astype(v.dtype), vb)
    return o.reshape(b, l, SWA_WIDTH)


def _complex_affine_combine(left, right):
    ar1, ai1, br1, bi1 = left
    ar2, ai2, br2, bi2 = right
    return (ar2 * ar1 - ai2 * ai1,
            ar2 * ai1 + ai2 * ar1,
            ar2 * br1 - ai2 * bi1 + br2,
            ar2 * bi1 + ai2 * br1 + bi2)


def s5_mixer(u, a_re, a_im, b_re, b_im, c_re, c_im, d_skip, log_dt, w_glu, b_glu):
    f32 = jnp.float32
    b, l, _ = u.shape
    uf = u.astype(f32)
    ug = uf.reshape(b, l, SSM_GROUPS, SSM_GROUP_CH)
    a_re, a_im = a_re.astype(f32), a_im.astype(f32)
    b_re, b_im = b_re.astype(f32), b_im.astype(f32)
    c_re, c_im = c_re.astype(f32), c_im.astype(f32)
    dt = jnp.exp(log_dt.astype(f32))[:, None]
    mag = jnp.exp(a_re * dt)
    lam_re = mag * jnp.cos(a_im * dt)
    lam_im = mag * jnp.sin(a_im * dt)
    den = a_re * a_re + a_im * a_im
    w_re = ((lam_re - 1.0) * a_re + lam_im * a_im) / den
    w_im = (lam_im * a_re - (lam_re - 1.0) * a_im) / den
    bb_re = w_re[..., None] * b_re - w_im[..., None] * b_im
    bb_im = w_re[..., None] * b_im + w_im[..., None] * b_re
    bu_re = jnp.einsum('blgp,gnp->blgn', ug, bb_re)
    bu_im = jnp.einsum('blgp,gnp->blgn', ug, bb_im)
    shape = (1, l, SSM_GROUPS, SSM_STATE)
    lr = jnp.broadcast_to(lam_re, shape)
    li = jnp.broadcast_to(lam_im, shape)
    _, _, h_re, h_im = lax.associative_scan(_complex_affine_combine, (lr, li, bu_re, bu_im), axis=1)
    y = jnp.einsum('blgn,gpn->blgp', h_re, c_re) - jnp.einsum('blgn,gpn->blgp', h_im, c_im)
    y = y.reshape(b, l, SSM_WIDTH) + d_skip.astype(f32) * uf
    y = jax.nn.gelu(y)
    out = y * jax.nn.sigmoid(y @ w_glu.astype(f32) + b_glu.astype(f32))
    return out.astype(u.dtype)


def stick_breaking_attention(q, k, v):
    b, l, h, d = q.shape
    nb = l // SB_BLOCK
    qb = q.reshape(b, nb, SB_BLOCK, h, d).transpose(1, 0, 3, 2, 4)
    kt = k.transpose(0, 2, 1, 3)
    vt = v.transpose(0, 2, 1, 3)
    kpos = jnp.arange(l)
    scale = d ** -0.5

    def block(args):
        q_blk, start = args
        z = jnp.einsum('bhqd,bhkd->bhqk', q_blk, kt).astype(jnp.float32) * scale
        qpos = start + jnp.arange(SB_BLOCK)
        mask = kpos[None, :] < qpos[:, None]
        log_keep = jnp.where(mask, jax.nn.log_sigmoid(-z), 0.0)
        log_survive = lax.cumsum(log_keep, axis=3, reverse=True) - log_keep
        weights = jnp.where(mask, jnp.exp(jax.nn.log_sigmoid(z) + log_survive), 0.0)
        return jnp.einsum('bhqk,bhkd->bhqd', weights.astype(vt.dtype), vt)

    starts = jnp.arange(nb, dtype=jnp.int32) * SB_BLOCK
    o = lax.map(block, (qb, starts))
    return o.transpose(1, 0, 3, 2, 4).reshape(b, l, h * d)


def _normal(key, shape, scale):
    return jax.random.normal(key, shape, jnp.float32) * scale


def _gain(key, shape):
    return 1.0 + 0.02 * jax.random.normal(key, shape, jnp.float32)


def setup_inputs(seed: int = 0) -> dict:
    key = jax.random.key(seed)
    ks = jax.random.split(key, 26)
    E, O, L = N_EVEN, N_ODD, DEPTH
    G, N, P = SSM_GROUPS, SSM_STATE, SSM_GROUP_CH
    n = jnp.arange(N, dtype=jnp.float32)
    log_dt = jax.random.uniform(ks[13], (E, G), jnp.float32,
                                math.log(DT_MIN), math.log(DT_MAX))
    return {
        "x": _normal(ks[0], (BATCH, SEQ, D_MODEL), 1.0),
        "even_norm": _gain(ks[1], (E, D_MODEL)),
        "even_w_in": _normal(ks[2], (E, D_MODEL, EVEN_IN), D_MODEL ** -0.5),
        "q_norm": _gain(ks[3], (E, HEAD_DIM)),
        "k_norm": _gain(ks[4], (E, HEAD_DIM)),
        "sinks": _normal(ks[5], (E, SWA_Q_HEADS), 0.5),
        "ssm_a_re": -0.5 * jnp.exp(0.02 * jax.random.normal(ks[6], (E, G, N), jnp.float32)),
        "ssm_a_im": jnp.pi * n + 0.01 * jax.random.normal(ks[7], (E, G, N), jnp.float32),
        "ssm_b_re": _normal(ks[8], (E, G, N, P), (2 * P) ** -0.5),
        "ssm_b_im": _normal(ks[9], (E, G, N, P), (2 * P) ** -0.5),
        "ssm_c_re": _normal(ks[10], (E, G, P, N), N ** -0.5),
        "ssm_c_im": _normal(ks[11], (E, G, P, N), N ** -0.5),
        "ssm_d": _normal(ks[12], (E, SSM_WIDTH), 0.5),
        "ssm_log_dt": log_dt,
        "ssm_w_glu": _normal(ks[14], (E, SSM_WIDTH, SSM_WIDTH), SSM_WIDTH ** -0.5),
        "ssm_b_glu": _normal(ks[15], (E, SSM_WIDTH), 0.02),
        "even_w_out": _normal(ks[16], (E, EVEN_MIX, D_MODEL), EVEN_MIX ** -0.5),
        "odd_norm": _gain(ks[17], (O, D_MODEL)),
        "odd_w_in": _normal(ks[18], (O, D_MODEL, 3 * SB_WIDTH), D_MODEL ** -0.5),
        "odd_w_out": _normal(ks[19], (O, SB_WIDTH, D_MODEL), SB_WIDTH ** -0.5),
        "ffn_norm": _gain(ks[20], (L, D_MODEL)),
        "ffn_w_gate": _normal(ks[21], (L, D_MODEL, D_FF), D_MODEL ** -0.5),
        "ffn_w_up": _normal(ks[22], (L, D_MODEL, D_FF), D_MODEL ** -0.5),
        "ffn_w_down": _normal(ks[23], (L, D_FF, D_MODEL), D_FF ** -0.5),
    }


def reference(x, even_norm, even_w_in, q_norm, k_norm, sinks,
              ssm_a_re, ssm_a_im, ssm_b_re, ssm_b_im, ssm_c_re, ssm_c_im,
              ssm_d, ssm_log_dt, ssm_w_glu, ssm_b_glu, even_w_out,
              odd_norm, odd_w_in, odd_w_out,
              ffn_norm, ffn_w_gate, ffn_w_up, ffn_w_down):
    b, l, _ = x.shape
    for layer in range(DEPTH):
        i = layer // 2
        if layer % 2 == 0:
            hn = rmsnorm(x, even_norm[i])
            proj = hn @ even_w_in[i]
            q, k, v, u = jnp.split(
                proj, [SWA_WIDTH, SWA_WIDTH + KV_WIDTH, SWA_WIDTH + 2 * KV_WIDTH], axis=-1)
            q = rmsnorm(q.reshape(b, l, SWA_Q_HEADS, HEAD_DIM), q_norm[i])
            k = rmsnorm(k.reshape(b, l, SWA_KV_HEADS, HEAD_DIM), k_norm[i])
            v = v.reshape(b, l, SWA_KV_HEADS, HEAD_DIM)
            o_attn = swa_sink_attention(q, k, v, sinks[i])
            o_ssm = s5_mixer(u, ssm_a_re[i], ssm_a_im[i], ssm_b_re[i], ssm_b_im[i],
                             ssm_c_re[i], ssm_c_im[i], ssm_d[i], ssm_log_dt[i],
                             ssm_w_glu[i], ssm_b_glu[i])
            mixed = jnp.concatenate([o_attn, o_ssm], axis=-1) @ even_w_out[i]
        else:
            hn = rmsnorm(x, odd_norm[i])
            q, k, v = jnp.split(hn @ odd_w_in[i], 3, axis=-1)
            q = q.reshape(b, l, SB_HEADS, HEAD_DIM)
            k = k.reshape(b, l, SB_HEADS, HEAD_DIM)
            v = v.reshape(b, l, SB_HEADS, HEAD_DIM)
            mixed = stick_breaking_attention(q, k, v) @ odd_w_out[i]
        x = x + mixed
        hn = rmsnorm(x, ffn_norm[layer])
        x = x + (jax.nn.silu(hn @ ffn_w_gate[layer]) * (hn @ ffn_w_up[layer])) @ ffn_w_down[layer]
    return x
```

```python
import functools
import math

import jax
import jax.numpy as jnp
from jax import lax
from jax.experimental import pallas as pl
from jax.experimental.pallas import tpu as pltpu

F32 = jnp.float32
BF16 = jnp.bfloat16

D_MODEL = 1024
HEAD_DIM = 64
EPS = 1e-6
LANES = 128

SWA_Q_HEADS = 8
SWA_KV_HEADS = 2
SWA_BLOCK = 128
SWA_WIDTH = SWA_Q_HEADS * HEAD_DIM
KV_WIDTH = SWA_KV_HEADS * HEAD_DIM

SSM_WIDTH = D_MODEL // 2
SSM_GROUP_CH = 16
SSM_GROUPS = SSM_WIDTH // SSM_GROUP_CH
SSM_STATE = 64
SSM_STATES = SSM_GROUPS * SSM_STATE
SSM_CHUNKS = SSM_WIDTH // LANES
SSM_CHUNK_STATES = SSM_STATES // SSM_CHUNKS
EVEN_IN = SWA_WIDTH + 2 * KV_WIDTH + SSM_WIDTH

SB_HEADS = D_MODEL // HEAD_DIM
SB_WIDTH = SB_HEADS * HEAD_DIM

VMEM_LIMIT_BYTES = 56 * 1024 * 1024


def _params(*sem):
    return pltpu.CompilerParams(dimension_semantics=sem, vmem_limit_bytes=VMEM_LIMIT_BYTES)


def _const_spec(shape):
    nd = len(shape)
    return pl.BlockSpec(shape, lambda *_: (0,) * nd, pipeline_mode=pl.Buffered(1))


def _rmsnorm(x, g):
    ms = jnp.mean(x * x, axis=-1, keepdims=True)
    return x * lax.rsqrt(ms + EPS) * g


def _pair_rmsnorm(x, g2):
    lo = lax.broadcasted_iota(jnp.int32, x.shape, 1) < HEAD_DIM
    sq = x * x
    s_lo = jnp.sum(jnp.where(lo, sq, 0.0), axis=-1, keepdims=True)
    s_hi = jnp.sum(jnp.where(lo, 0.0, sq), axis=-1, keepdims=True)
    ms = jnp.where(lo, s_lo, s_hi) * (1.0 / HEAD_DIM)
    return x * lax.rsqrt(ms + EPS) * g2


def _even_in_kernel(x_ref, g_ref, w_ref, qg_ref, kg_ref, q_ref, k_ref, v_ref, u_ref):
    hn = _rmsnorm(x_ref[0], g_ref[...]).astype(BF16)
    proj = jnp.dot(hn, w_ref[...], preferred_element_type=F32)
    scale = HEAD_DIM ** -0.5
    for p in range(SWA_WIDTH // LANES):
        qp = _pair_rmsnorm(proj[:, p * LANES:(p + 1) * LANES], qg_ref[...])
        q_ref[0, :, p * LANES:(p + 1) * LANES] = (qp * scale).astype(BF16)
    k = _pair_rmsnorm(proj[:, SWA_WIDTH:SWA_WIDTH + KV_WIDTH], kg_ref[...])
    k_ref[0] = k.astype(BF16)
    v_ref[0] = proj[:, SWA_WIDTH + KV_WIDTH:SWA_WIDTH + 2 * KV_WIDTH].astype(BF16)
    u_ref[...] = proj[:, SWA_WIDTH + 2 * KV_WIDTH:]


def _even_in_proj(x, g, w, qg, kg, tm):
    b, l, d = x.shape
    return pl.pallas_call(
        _even_in_kernel,
        grid=(b, l // tm),
        in_specs=[
            pl.BlockSpec((1, tm, d), lambda i, j: (i, j, 0)),
            _const_spec((1, d)),
            _const_spec((d, EVEN_IN)),
            _const_spec((1, LANES)),
            _const_spec((1, LANES)),
        ],
        out_specs=[
            pl.BlockSpec((1, tm, SWA_WIDTH), lambda i, j: (i, j, 0)),
            pl.BlockSpec((1, tm, KV_WIDTH), lambda i, j: (i, j, 0)),
            pl.BlockSpec((1, tm, KV_WIDTH), lambda i, j: (i, j, 0)),
            pl.BlockSpec((tm, SSM_WIDTH), lambda i, j: (j, i)),
        ],
        out_shape=[
            jax.ShapeDtypeStruct((b, l, SWA_WIDTH), BF16),
            jax.ShapeDtypeStruct((b, l, KV_WIDTH), BF16),
            jax.ShapeDtypeStruct((b, l, KV_WIDTH), BF16),
            jax.ShapeDtypeStruct((l, b * SSM_WIDTH), F32),
        ],
        compiler_params=_params("parallel", "parallel"),
        name="even_in_proj",
    )(x, g, w, qg, kg)


def _swa_kernel(sink_ref, q_ref, kc_ref, kp_ref, vc_ref, vp_ref, o_ref):
    n = pl.program_id(1)
    blk = SWA_BLOCK
    kcat = jnp.concatenate([kp_ref[0], kc_ref[0]], axis=0).astype(F32)
    vcat = jnp.concatenate([vp_ref[0], vc_ref[0]], axis=0).astype(F32)
    lo = lax.broadcasted_iota(jnp.int32, kcat.shape, 1) < HEAD_DIM

    def halves(t):
        g0_lo = jnp.where(lo, t, 0.0)
        g1_hi = jnp.where(lo, 0.0, t)
        g0_hi = pltpu.roll(g0_lo, HEAD_DIM, 1)
        g1_lo = pltpu.roll(g1_hi, HEAD_DIM, 1)
        return [[g0_lo.astype(BF16), g0_hi.astype(BF16)], [g1_lo.astype(BF16), g1_hi.astype(BF16)]]

    ks, vs = halves(kcat), halves(vcat)
    qq = lax.broadcasted_iota(jnp.int32, (blk, 2 * blk), 0)
    kk = lax.broadcasted_iota(jnp.int32, (blk, 2 * blk), 1)
    diff = qq + blk - kk
    first_key = jnp.where(n > 0, 0, blk)
    mask = (diff >= 0) & (diff < blk) & (kk >= first_key)
    for p in range(SWA_WIDTH // LANES):
        qp = q_ref[0, :, p * LANES:(p + 1) * LANES]
        g = (2 * p) // (SWA_Q_HEADS // SWA_KV_HEADS)
        acc = None
        for e in range(2):
            sink = sink_ref[2 * p + e]
            z = lax.dot_general(qp, ks[g][e], (((1,), (1,)), ((), ())), preferred_element_type=F32)
            s = jnp.where(mask, z, -jnp.inf)
            m = jnp.maximum(jnp.max(s, axis=-1, keepdims=True), sink)
            pe = jnp.exp(s - m)
            den = jnp.sum(pe, axis=-1, keepdims=True) + jnp.exp(sink - m)
            o = jnp.dot(pe.astype(BF16), vs[g][e], preferred_element_type=F32) / den
            acc = o if acc is None else acc + o
        o_ref[0, :, p * LANES:(p + 1) * LANES] = acc.astype(BF16)


def _swa_attention(sinks, q, k, v):
    b, l, _ = q.shape
    blk = SWA_BLOCK
    cur = lambda i, j: (i, j, 0)
    prev = lambda i, j: (i, jnp.maximum(j - 1, 0), 0)
    return pl.pallas_call(
        _swa_kernel,
        grid=(b, l // blk),
        in_specs=[
            pl.BlockSpec(memory_space=pltpu.SMEM),
            pl.BlockSpec((1, blk, SWA_WIDTH), cur),
            pl.BlockSpec((1, blk, KV_WIDTH), cur),
            pl.BlockSpec((1, blk, KV_WIDTH), prev),
            pl.BlockSpec((1, blk, KV_WIDTH), cur),
            pl.BlockSpec((1, blk, KV_WIDTH), prev),
        ],
        out_specs=pl.BlockSpec((1, blk, SWA_WIDTH), cur),
        out_shape=jax.ShapeDtypeStruct((b, l, SWA_WIDTH), BF16),
        compiler_params=_params("parallel", "parallel"),
        name="swa_attention",
    )(sinks, q, k, k, v, v)


def _ssm_disc_kernel(are_ref, aim_ref, ldt_ref, lre_ref, lim_ref, wre_ref, wim_ref):
    a_re, a_im = are_ref[...], aim_ref[...]
    dt = jnp.exp(ldt_ref[...])
    mag = jnp.exp(a_re * dt)
    lam_re = mag * jnp.cos(a_im * dt)
    lam_im = mag * jnp.sin(a_im * dt)
    den = a_re * a_re + a_im * a_im
    lre_ref[...] = lam_re
    lim_ref[...] = lam_im
    wre_ref[...] = ((lam_re - 1.0) * a_re + lam_im * a_im) / den
    wim_ref[...] = (lam_im * a_re - (lam_re - 1.0) * a_im) / den


def _ssm_bbar_kernel(wre_ref, wim_ref, bre_ref, bim_ref, ore_ref, oim_ref):
    w_re, w_im, b_re, b_im = wre_ref[...], wim_ref[...], bre_ref[...], bim_ref[...]
    ore_ref[...] = w_re * b_re - w_im * b_im
    oim_ref[...] = w_re * b_im + w_im * b_re


def _block_diag(t):
    ch, gl, r, c = t.shape
    eye = jnp.eye(gl, dtype=t.dtype)
    return (t[:, :, :, None, :] * eye[None, :, None, :, None]).reshape(ch, gl * r, gl * c)


def _ssm_prepare(a_re, a_im, b_re, b_im, c_re, c_im, log_dt, batch):
    g, n, p = SSM_GROUPS, SSM_STATE, SSM_GROUP_CH
    gn = jax.ShapeDtypeStruct((g, n), F32)
    ldt = jnp.broadcast_to(log_dt[:, None], (g, n))
    lam_re, lam_im, w_re, w_im = pl.pallas_call(
        _ssm_disc_kernel, out_shape=[gn, gn, gn, gn], name="ssm_discretise")(a_re, a_im, ldt)
    gnp = jax.ShapeDtypeStruct((g, n * p), F32)
    bb_re, bb_im = pl.pallas_call(_ssm_bbar_kernel, out_shape=[gnp, gnp], name="ssm_bbar")(
        jnp.repeat(w_re, p, axis=1), jnp.repeat(w_im, p, axis=1),
        b_re.reshape(g, n * p), b_im.reshape(g, n * p))
    gl = g // SSM_CHUNKS
    to_in = lambda t: _block_diag(
        t.reshape(SSM_CHUNKS, gl, n, p).transpose(0, 1, 3, 2)).astype(BF16)
    to_out = lambda t: _block_diag(
        t.reshape(SSM_CHUNKS, gl, p, n).transpose(0, 1, 3, 2)).astype(BF16)
    bcast = lambda t: jnp.broadcast_to(t.reshape(1, SSM_STATES), (batch, SSM_STATES))
    return (to_in(bb_re), to_in(bb_im), bcast(lam_re), bcast(lam_im), to_out(c_re), to_out(c_im))


def _ssm_kernel(u_ref, bre_ref, bim_ref, lre_ref, lim_ref, cre_ref, cim_ref, d_ref, wg_ref, bg_ref,
                o_ref, hre_ref, him_ref, sre_ref, sim_ref, *, batch, steps):
    @pl.when(pl.program_id(0) == 0)
    def _():
        sre_ref[...] = jnp.zeros_like(sre_ref)
        sim_ref[...] = jnp.zeros_like(sim_ref)

    u = u_ref[...]
    ub = u.astype(BF16)
    cs = SSM_CHUNK_STATES
    for c in range(SSM_CHUNKS):
        uc = ub[:, c * LANES:(c + 1) * LANES]
        hre_ref[:, c * cs:(c + 1) * cs] = jnp.dot(uc, bre_ref[c], preferred_element_type=F32)
        him_ref[:, c * cs:(c + 1) * cs] = jnp.dot(uc, bim_ref[c], preferred_element_type=F32)

    def step(t, carry):
        h_re, h_im = carry
        rows = pl.ds(pl.multiple_of(t * batch, batch), batch)
        l_re, l_im = lre_ref[...], lim_ref[...]
        n_re = l_re * h_re - l_im * h_im + hre_ref[rows, :]
        n_im = l_re * h_im + l_im * h_re + him_ref[rows, :]
        hre_ref[rows, :] = n_re
        him_ref[rows, :] = n_im
        return n_re, n_im

    h_re, h_im = lax.fori_loop(0, steps, step, (sre_ref[...], sim_ref[...]))
    sre_ref[...] = h_re
    sim_ref[...] = h_im

    ys = []
    for c in range(SSM_CHUNKS):
        hr = hre_ref[:, c * cs:(c + 1) * cs].astype(BF16)
        hi = him_ref[:, c * cs:(c + 1) * cs].astype(BF16)
        ys.append(jnp.dot(hr, cre_ref[c], preferred_element_type=F32)
                  - jnp.dot(hi, cim_ref[c], preferred_element_type=F32))
    y = jnp.concatenate(ys, axis=-1) + d_ref[...] * u
    y = jax.nn.gelu(y)
    gate = jnp.dot(y.astype(BF16), wg_ref[...], preferred_element_type=F32) + bg_ref[...]
    o_ref[...] = (y * jax.nn.sigmoid(gate)).astype(BF16)


def _ssm_mixer(u_tb, mats, d_skip, w_glu, b_glu, batch, steps):
    rows_total, width = u_tb.shape
    rows = steps * batch
    b_in_re, b_in_im, lam_re, lam_im, c_out_re, c_out_im = mats
    return pl.pallas_call(
        functools.partial(_ssm_kernel, batch=batch, steps=steps),
        grid=(rows_total // rows,),
        in_specs=[
            pl.BlockSpec((rows, width), lambda i: (i, 0)),
            _const_spec(b_in_re.shape), _const_spec(b_in_im.shape),
            _const_spec(lam_re.shape), _const_spec(lam_im.shape),
            _const_spec(c_out_re.shape), _const_spec(c_out_im.shape),
            _const_spec((1, width)), _const_spec((width, width)), _const_spec((1, width)),
        ],
        out_specs=pl.BlockSpec((rows, width), lambda i: (i, 0)),
        out_shape=jax.ShapeDtypeStruct((rows_total, width), BF16),
        scratch_shapes=[
            pltpu.VMEM((rows, SSM_STATES), F32), pltpu.VMEM((rows, SSM_STATES), F32),
            pltpu.VMEM((batch, SSM_STATES), F32), pltpu.VMEM((batch, SSM_STATES), F32),
        ],
        compiler_params=_params("arbitrary"),
        name="ssm_mixer",
    )(u_tb, b_in_re, b_in_im, lam_re, lam_im, c_out_re, c_out_im, d_skip, w_glu, b_glu)


def _odd_in_kernel(x_ref, g_ref, w_ref, q_ref, k_ref, v_ref):
    hn = _rmsnorm(x_ref[0], g_ref[...]).astype(BF16)
    proj = jnp.dot(hn, w_ref[...], preferred_element_type=F32)
    q_ref[0] = (proj[:, :SB_WIDTH] * (HEAD_DIM ** -0.5)).astype(BF16)
    k_ref[0] = proj[:, SB_WIDTH:2 * SB_WIDTH].astype(BF16)
    v_ref[0] = proj[:, 2 * SB_WIDTH:].astype(BF16)


def _odd_in_proj(x, g, w, tm):
    b, l, d = x.shape
    row = pl.BlockSpec((1, tm, SB_WIDTH), lambda i, j: (i, j, 0))
    out = jax.ShapeDtypeStruct((b, l, SB_WIDTH), BF16)
    return pl.pallas_call(
        _odd_in_kernel,
        grid=(b, l // tm),
        in_specs=[pl.BlockSpec((1, tm, d), lambda i, j: (i, j, 0)),
                  _const_spec((1, d)), _const_spec((d, 3 * SB_WIDTH))],
        out_specs=[row, row, row],
        out_shape=[out, out, out],
        compiler_params=_params("parallel", "parallel"),
        name="odd_in_proj",
    )(x, g, w)


def _sb_kernel(q_ref, k_ref, v_ref, tri_ref, o_ref, c_ref, acc_ref, *, tq, tk):
    qi = pl.program_id(2)
    q = q_ref[0]
    n_diag = tq // tk
    n_blocks = (qi + 1) * n_diag
    lo_k = lax.broadcasted_iota(jnp.int32, (tk, LANES), 1) < HEAD_DIM
    lo_q = lax.broadcasted_iota(jnp.int32, (tq, LANES), 1) < HEAD_DIM
    c_ref[...] = jnp.zeros_like(c_ref)
    acc_ref[...] = jnp.zeros_like(acc_ref)
    tri = tri_ref[...]

    def visit(j, masked):
        k0 = pl.multiple_of(j * tk, tk)
        kb = k_ref[0, pl.ds(k0, tk), :]
        vb = v_ref[0, pl.ds(k0, tk), :]
        if masked:
            row = qi * tq + lax.broadcasted_iota(jnp.int32, (tq, tk), 0)
            col = k0 + lax.broadcasted_iota(jnp.int32, (tq, tk), 1)
            valid = col < row
        for h in range(2):
            kh = jnp.where(lo_k, kb, 0) if h == 0 else jnp.where(lo_k, 0, kb)
            z = lax.dot_general(q, kh.astype(BF16), (((1,), (1,)), ((), ())),
                                preferred_element_type=F32)
            sp = jnp.maximum(z, 0.0) + jnp.log(1.0 + jnp.exp(-jnp.abs(z)))
            if masked:
                sp = jnp.where(valid, sp, 0.0)
            sp_hi = sp.astype(BF16)
            sp_lo = (sp - sp_hi.astype(F32)).astype(BF16)
            w = (jnp.dot(sp_hi, tri, preferred_element_type=F32)
                 + jnp.dot(sp_lo, tri, preferred_element_type=F32))
            p = jnp.exp(z - w)
            if masked:
                p = jnp.where(valid, p, 0.0)
            pv = jnp.dot(p.astype(BF16), vb, preferred_element_type=F32)
            c = c_ref[h]
            acc_ref[h] += pv * jnp.exp(-c)
            c_ref[h] = c + jnp.broadcast_to(w[:, 0:1], (tq, LANES))

    for d in range(n_diag):
        visit(n_blocks - 1 - d, True)

    def body(i, carry):
        visit(n_blocks - n_diag - 1 - i, False)
        return carry

    lax.fori_loop(0, n_blocks - n_diag, body, 0)
    o_ref[0] = jnp.where(lo_q, acc_ref[0], acc_ref[1]).astype(BF16)


def _sb_attention(q, k, v, tq, tk):
    b, l, width = q.shape
    tri = (jnp.arange(tk)[:, None] >= jnp.arange(tk)[None, :]).astype(BF16)
    return pl.pallas_call(
        functools.partial(_sb_kernel, tq=tq, tk=tk),
        grid=(b, width // LANES, l // tq),
        in_specs=[
            pl.BlockSpec((1, tq, LANES), lambda i, h, j: (i, j, h)),
            pl.BlockSpec((1, l, LANES), lambda i, h, j: (i, 0, h)),
            pl.BlockSpec((1, l, LANES), lambda i, h, j: (i, 0, h)),
            _const_spec((tk, tk)),
        ],
        out_specs=pl.BlockSpec((1, tq, LANES), lambda i, h, j: (i, j, h)),
        out_shape=jax.ShapeDtypeStruct((b, l, width), BF16),
        scratch_shapes=[pltpu.VMEM((2, tq, LANES), F32), pltpu.VMEM((2, tq, LANES), F32)],
        compiler_params=_params("parallel", "parallel", "arbitrary"),
        name="sb_attention",
    )(q, k, v, tri)


def _post_kernel(*refs, n_mix):
    x_ref = refs[0]
    mix_refs = refs[1:1 + n_mix]
    wo_refs = refs[1 + n_mix:1 + 2 * n_mix]
    g_ref, wg_ref, wu_ref, wd_ref, o_ref = refs[1 + 2 * n_mix:]
    x = x_ref[0]
    for m_ref, w_ref in zip(mix_refs, wo_refs):
        m = m_ref[0] if len(m_ref.shape) == 3 else m_ref[...]
        x = x + jnp.dot(m, w_ref[...], preferred_element_type=F32)
    hn = _rmsnorm(x, g_ref[...]).astype(BF16)
    gate = jnp.dot(hn, wg_ref[...], preferred_element_type=F32)
    up = jnp.dot(hn, wu_ref[...], preferred_element_type=F32)
    act = (gate * jax.nn.sigmoid(gate) * up).astype(BF16)
    o_ref[0] = x + jnp.dot(act, wd_ref[...], preferred_element_type=F32)


def _post(x, mixes, mix_specs, w_outs, g, w_gate, w_up, w_down, tm):
    b, l, d = x.shape
    xspec = pl.BlockSpec((1, tm, d), lambda i, j: (i, j, 0))
    return pl.pallas_call(
        functools.partial(_post_kernel, n_mix=len(mixes)),
        grid=(b, l // tm),
        in_specs=[xspec, *mix_specs, *[_const_spec(w.shape) for w in w_outs],
                  _const_spec((1, d)), _const_spec(w_gate.shape), _const_spec(w_up.shape),
                  _const_spec(w_down.shape)],
        out_specs=xspec,
        out_shape=jax.ShapeDtypeStruct(x.shape, F32),
        compiler_params=_params("parallel", "parallel"),
        name="out_proj_ffn",
    )(x, *mixes, *w_outs, g, w_gate, w_up, w_down)


def _tiles(l):
    tm = min(512, l)
    tq = min(512, l)
    tk = min(256, tq)
    steps = min(64, l)
    return tm, tq, tk, steps


def kernel(x, even_norm, even_w_in, q_norm, k_norm, sinks, ssm_a_re, ssm_a_im, ssm_b_re, ssm_b_im,
           ssm_c_re, ssm_c_im, ssm_d, ssm_log_dt, ssm_w_glu, ssm_b_glu, even_w_out, odd_norm,
           odd_w_in, odd_w_out, ffn_norm, ffn_w_gate, ffn_w_up, ffn_w_down):
    b, l, d = x.shape
    assert d == D_MODEL and l % SWA_BLOCK == 0
    tm, tq, tk, steps = _tiles(l)
    depth = ffn_norm.shape[0]
    bf = lambda t: t.astype(BF16)
    row = lambda t: t.reshape(1, -1)
    for layer in range(depth):
        i = layer // 2
        if layer % 2 == 0:
            q, k, v, u_t = _even_in_proj(
                x, row(even_norm[i]), bf(even_w_in[i]),
                row(jnp.tile(q_norm[i], 2)), row(jnp.tile(k_norm[i], 2)), tm)
            o_attn = _swa_attention(sinks[i], q, k, v)
            mats = _ssm_prepare(ssm_a_re[i], ssm_a_im[i], ssm_b_re[i], ssm_b_im[i],
                                ssm_c_re[i], ssm_c_im[i], ssm_log_dt[i], b)
            o_ssm = _ssm_mixer(u_t.reshape(l * b, SSM_WIDTH), mats, row(ssm_d[i]),
                               bf(ssm_w_glu[i]), row(ssm_b_glu[i]), b, steps)
            mixes = [o_attn, o_ssm.reshape(l, b * SSM_WIDTH)]
            mix_specs = [pl.BlockSpec((1, tm, SWA_WIDTH), lambda bi, j: (bi, j, 0)),
                         pl.BlockSpec((tm, SSM_WIDTH), lambda bi, j: (j, bi))]
            w_outs = [bf(even_w_out[i][:SWA_WIDTH]), bf(even_w_out[i][SWA_WIDTH:])]
        else:
            q, k, v = _odd_in_proj(x, row(odd_norm[i]), bf(odd_w_in[i]), tm)
            mixes = [_sb_attention(q, k, v, tq, tk)]
            mix_specs = [pl.BlockSpec((1, tm, SB_WIDTH), lambda bi, j: (bi, j, 0))]
            w_outs = [bf(odd_w_out[i])]
        x = _post(x, mixes, mix_specs, w_outs, row(ffn_norm[layer]), bf(ffn_w_gate[layer]),
                  bf(ffn_w_up[layer]), bf(ffn_w_down[layer]), tm)
    return x
```

```python
import functools
import math

import jax
import jax.numpy as jnp
from jax import lax
from jax.experimental import pallas as pl
from jax.experimental.pallas import tpu as pltpu

F32 = jnp.float32
BF16 = jnp.bfloat16

D_MODEL = 1024
HEAD_DIM = 64
EPS = 1e-6
LANES = 128

SWA_Q_HEADS = 8
SWA_KV_HEADS = 2
SWA_BLOCK = 128
SWA_WIDTH = SWA_Q_HEADS * HEAD_DIM
KV_WIDTH = SWA_KV_HEADS * HEAD_DIM

SSM_WIDTH = D_MODEL // 2
SSM_GROUP_CH = 16
SSM_GROUPS = SSM_WIDTH // SSM_GROUP_CH
SSM_STATE = 64
SSM_STATES = SSM_GROUPS * SSM_STATE
SSM_CHUNKS = SSM_WIDTH // LANES
SSM_CHUNK_STATES = SSM_STATES // SSM_CHUNKS
EVEN_IN = SWA_WIDTH + 2 * KV_WIDTH + SSM_WIDTH

SB_HEADS = D_MODEL // HEAD_DIM
SB_WIDTH = SB_HEADS * HEAD_DIM
LOG2E = math.log2(math.e)
SB_DEAD_LOG2 = 160.0

VMEM_LIMIT_BYTES = 56 * 1024 * 1024


def _params(*sem):
    return pltpu.CompilerParams(dimension_semantics=sem, vmem_limit_bytes=VMEM_LIMIT_BYTES)


def _const_spec(shape):
    nd = len(shape)
    return pl.BlockSpec(shape, lambda *_: (0,) * nd, pipeline_mode=pl.Buffered(1))


def _rmsnorm(x, g):
    ms = jnp.mean(x * x, axis=-1, keepdims=True)
    return x * lax.rsqrt(ms + EPS) * g


def _pair_rmsnorm(x, g2):
    lo = lax.broadcasted_iota(jnp.int32, x.shape, 1) < HEAD_DIM
    sq = x * x
    s_lo = jnp.sum(jnp.where(lo, sq, 0.0), axis=-1, keepdims=True)
    s_hi = jnp.sum(jnp.where(lo, 0.0, sq), axis=-1, keepdims=True)
    ms = jnp.where(lo, s_lo, s_hi) * (1.0 / HEAD_DIM)
    return x * lax.rsqrt(ms + EPS) * g2


def _even_in_kernel(x_ref, g_ref, w_ref, qg_ref, kg_ref, q_ref, k_ref, v_ref, u_ref):
    hn = _rmsnorm(x_ref[0], g_ref[...]).astype(BF16)
    proj = jnp.dot(hn, w_ref[...], preferred_element_type=F32)
    scale = HEAD_DIM ** -0.5
    for p in range(SWA_WIDTH // LANES):
        qp = _pair_rmsnorm(proj[:, p * LANES:(p + 1) * LANES], qg_ref[...])
        q_ref[0, :, p * LANES:(p + 1) * LANES] = (qp * scale).astype(BF16)
    k = _pair_rmsnorm(proj[:, SWA_WIDTH:SWA_WIDTH + KV_WIDTH], kg_ref[...])
    k_ref[0] = k.astype(BF16)
    v_ref[0] = proj[:, SWA_WIDTH + KV_WIDTH:SWA_WIDTH + 2 * KV_WIDTH].astype(BF16)
    u_ref[...] = proj[:, SWA_WIDTH + 2 * KV_WIDTH:]


def _even_in_proj(x, g, w, qg, kg, tm):
    b, l, d = x.shape
    return pl.pallas_call(
        _even_in_kernel,
        grid=(b, l // tm),
        in_specs=[
            pl.BlockSpec((1, tm, d), lambda i, j: (i, j, 0)),
            _const_spec((1, d)),
            _const_spec((d, EVEN_IN)),
            _const_spec((1, LANES)),
            _const_spec((1, LANES)),
        ],
        out_specs=[
            pl.BlockSpec((1, tm, SWA_WIDTH), lambda i, j: (i, j, 0)),
            pl.BlockSpec((1, tm, KV_WIDTH), lambda i, j: (i, j, 0)),
            pl.BlockSpec((1, tm, KV_WIDTH), lambda i, j: (i, j, 0)),
            pl.BlockSpec((tm, SSM_WIDTH), lambda i, j: (j, i)),
        ],
        out_shape=[
            jax.ShapeDtypeStruct((b, l, SWA_WIDTH), BF16),
            jax.ShapeDtypeStruct((b, l, KV_WIDTH), BF16),
            jax.ShapeDtypeStruct((b, l, KV_WIDTH), BF16),
            jax.ShapeDtypeStruct((l, b * SSM_WIDTH), F32),
        ],
        compiler_params=_params("parallel", "parallel"),
        name="even_in_proj",
    )(x, g, w, qg, kg)


def _swa_kernel(sink_ref, q_ref, kc_ref, kp_ref, vc_ref, vp_ref, o_ref):
    n = pl.program_id(1)
    blk = SWA_BLOCK
    kcat = jnp.concatenate([kp_ref[0], kc_ref[0]], axis=0).astype(F32)
    vcat = jnp.concatenate([vp_ref[0], vc_ref[0]], axis=0).astype(F32)
    lo = lax.broadcasted_iota(jnp.int32, kcat.shape, 1) < HEAD_DIM

    def halves(t):
        g0_lo = jnp.where(lo, t, 0.0)
        g1_hi = jnp.where(lo, 0.0, t)
        g0_hi = pltpu.roll(g0_lo, HEAD_DIM, 1)
        g1_lo = pltpu.roll(g1_hi, HEAD_DIM, 1)
        return [[g0_lo.astype(BF16), g0_hi.astype(BF16)], [g1_lo.astype(BF16), g1_hi.astype(BF16)]]

    ks, vs = halves(kcat), halves(vcat)
    qq = lax.broadcasted_iota(jnp.int32, (blk, 2 * blk), 0)
    kk = lax.broadcasted_iota(jnp.int32, (blk, 2 * blk), 1)
    diff = qq + blk - kk
    first_key = jnp.where(n > 0, 0, blk)
    mask = (diff >= 0) & (diff < blk) & (kk >= first_key)
    for p in range(SWA_WIDTH // LANES):
        qp = q_ref[0, :, p * LANES:(p + 1) * LANES]
        g = (2 * p) // (SWA_Q_HEADS // SWA_KV_HEADS)
        acc = None
        for e in range(2):
            sink = sink_ref[2 * p + e]
            z = lax.dot_general(qp, ks[g][e], (((1,), (1,)), ((), ())), preferred_element_type=F32)
            s = jnp.where(mask, z, -jnp.inf)
            m = jnp.maximum(jnp.max(s, axis=-1, keepdims=True), sink)
            pe = jnp.exp(s - m)
            den = jnp.sum(pe, axis=-1, keepdims=True) + jnp.exp(sink - m)
            o = jnp.dot(pe.astype(BF16), vs[g][e], preferred_element_type=F32) / den
            acc = o if acc is None else acc + o
        o_ref[0, :, p * LANES:(p + 1) * LANES] = acc.astype(BF16)


def _swa_attention(sinks, q, k, v):
    b, l, _ = q.shape
    blk = SWA_BLOCK
    cur = lambda i, j: (i, j, 0)
    prev = lambda i, j: (i, jnp.maximum(j - 1, 0), 0)
    return pl.pallas_call(
        _swa_kernel,
        grid=(b, l // blk),
        in_specs=[
            pl.BlockSpec(memory_space=pltpu.SMEM),
            pl.BlockSpec((1, blk, SWA_WIDTH), cur),
            pl.BlockSpec((1, blk, KV_WIDTH), cur),
            pl.BlockSpec((1, blk, KV_WIDTH), prev),
            pl.BlockSpec((1, blk, KV_WIDTH), cur),
            pl.BlockSpec((1, blk, KV_WIDTH), prev),
        ],
        out_specs=pl.BlockSpec((1, blk, SWA_WIDTH), cur),
        out_shape=jax.ShapeDtypeStruct((b, l, SWA_WIDTH), BF16),
        compiler_params=_params("parallel", "parallel"),
        name="swa_attention",
    )(sinks, q, k, k, v, v)


def _ssm_disc_kernel(are_ref, aim_ref, ldt_ref, lre_ref, lim_ref, wre_ref, wim_ref):
    a_re, a_im = are_ref[...], aim_ref[...]
    dt = jnp.exp(ldt_ref[...])
    mag = jnp.exp(a_re * dt)
    lam_re = mag * jnp.cos(a_im * dt)
    lam_im = mag * jnp.sin(a_im * dt)
    den = a_re * a_re + a_im * a_im
    lre_ref[...] = lam_re
    lim_ref[...] = lam_im
    wre_ref[...] = ((lam_re - 1.0) * a_re + lam_im * a_im) / den
    wim_ref[...] = (lam_im * a_re - (lam_re - 1.0) * a_im) / den


def _ssm_bbar_kernel(wre_ref, wim_ref, bre_ref, bim_ref, ore_ref, oim_ref):
    w_re, w_im, b_re, b_im = wre_ref[...], wim_ref[...], bre_ref[...], bim_ref[...]
    ore_ref[...] = w_re * b_re - w_im * b_im
    oim_ref[...] = w_re * b_im + w_im * b_re


def _block_diag(t):
    ch, gl, r, c = t.shape
    eye = jnp.eye(gl, dtype=t.dtype)
    return (t[:, :, :, None, :] * eye[None, :, None, :, None]).reshape(ch, gl * r, gl * c)


def _ssm_prepare(a_re, a_im, b_re, b_im, c_re, c_im, log_dt, batch):
    g, n, p = SSM_GROUPS, SSM_STATE, SSM_GROUP_CH
    gn = jax.ShapeDtypeStruct((g, n), F32)
    ldt = jnp.broadcast_to(log_dt[:, None], (g, n))
    lam_re, lam_im, w_re, w_im = pl.pallas_call(
        _ssm_disc_kernel, out_shape=[gn, gn, gn, gn], name="ssm_discretise")(a_re, a_im, ldt)
    gnp = jax.ShapeDtypeStruct((g, n * p), F32)
    bb_re, bb_im = pl.pallas_call(_ssm_bbar_kernel, out_shape=[gnp, gnp], name="ssm_bbar")(
        jnp.repeat(w_re, p, axis=1), jnp.repeat(w_im, p, axis=1),
        b_re.reshape(g, n * p), b_im.reshape(g, n * p))
    gl = g // SSM_CHUNKS
    to_in = lambda t: _block_diag(
        t.reshape(SSM_CHUNKS, gl, n, p).transpose(0, 1, 3, 2)).astype(BF16)
    to_out = lambda t: _block_diag(
        t.reshape(SSM_CHUNKS, gl, p, n).transpose(0, 1, 3, 2)).astype(BF16)
    bcast = lambda t: jnp.broadcast_to(t.reshape(1, SSM_STATES), (batch, SSM_STATES))
    return (to_in(bb_re), to_in(bb_im), bcast(lam_re), bcast(lam_im), to_out(c_re), to_out(c_im))


def _ssm_kernel(u_ref, bre_ref, bim_ref, lre_ref, lim_ref, cre_ref, cim_ref, d_ref, wg_ref, bg_ref,
                o_ref, hre_ref, him_ref, sre_ref, sim_ref, *, batch, steps):
    @pl.when(pl.program_id(0) == 0)
    def _():
        sre_ref[...] = jnp.zeros_like(sre_ref)
        sim_ref[...] = jnp.zeros_like(sim_ref)

    u = u_ref[...]
    ub = u.astype(BF16)
    cs = SSM_CHUNK_STATES
    for c in range(SSM_CHUNKS):
        uc = ub[:, c * LANES:(c + 1) * LANES]
        hre_ref[:, c * cs:(c + 1) * cs] = jnp.dot(uc, bre_ref[c], preferred_element_type=F32)
        him_ref[:, c * cs:(c + 1) * cs] = jnp.dot(uc, bim_ref[c], preferred_element_type=F32)

    def step(t, carry):
        h_re, h_im = carry
        rows = pl.ds(pl.multiple_of(t * batch, batch), batch)
        l_re, l_im = lre_ref[...], lim_ref[...]
        n_re = l_re * h_re - l_im * h_im + hre_ref[rows, :]
        n_im = l_re * h_im + l_im * h_re + him_ref[rows, :]
        hre_ref[rows, :] = n_re
        him_ref[rows, :] = n_im
        return n_re, n_im

    h_re, h_im = lax.fori_loop(0, steps, step, (sre_ref[...], sim_ref[...]))
    sre_ref[...] = h_re
    sim_ref[...] = h_im

    ys = []
    for c in range(SSM_CHUNKS):
        hr = hre_ref[:, c * cs:(c + 1) * cs].astype(BF16)
        hi = him_ref[:, c * cs:(c + 1) * cs].astype(BF16)
        ys.append(jnp.dot(hr, cre_ref[c], preferred_element_type=F32)
                  - jnp.dot(hi, cim_ref[c], preferred_element_type=F32))
    y = jnp.concatenate(ys, axis=-1) + d_ref[...] * u
    y = jax.nn.gelu(y)
    gate = jnp.dot(y.astype(BF16), wg_ref[...], preferred_element_type=F32) + bg_ref[...]
    o_ref[...] = (y * jax.nn.sigmoid(gate)).astype(BF16)


def _ssm_mixer(u_tb, mats, d_skip, w_glu, b_glu, batch, steps):
    rows_total, width = u_tb.shape
    rows = steps * batch
    b_in_re, b_in_im, lam_re, lam_im, c_out_re, c_out_im = mats
    return pl.pallas_call(
        functools.partial(_ssm_kernel, batch=batch, steps=steps),
        grid=(rows_total // rows,),
        in_specs=[
            pl.BlockSpec((rows, width), lambda i: (i, 0)),
            _const_spec(b_in_re.shape), _const_spec(b_in_im.shape),
            _const_spec(lam_re.shape), _const_spec(lam_im.shape),
            _const_spec(c_out_re.shape), _const_spec(c_out_im.shape),
            _const_spec((1, width)), _const_spec((width, width)), _const_spec((1, width)),
        ],
        out_specs=pl.BlockSpec((rows, width), lambda i: (i, 0)),
        out_shape=jax.ShapeDtypeStruct((rows_total, width), BF16),
        scratch_shapes=[
            pltpu.VMEM((rows, SSM_STATES), F32), pltpu.VMEM((rows, SSM_STATES), F32),
            pltpu.VMEM((batch, SSM_STATES), F32), pltpu.VMEM((batch, SSM_STATES), F32),
        ],
        compiler_params=_params("arbitrary"),
        name="ssm_mixer",
    )(u_tb, b_in_re, b_in_im, lam_re, lam_im, c_out_re, c_out_im, d_skip, w_glu, b_glu)


def _odd_in_kernel(x_ref, g_ref, w_ref, q_ref, k_ref, v_ref):
    hn = _rmsnorm(x_ref[0], g_ref[...]).astype(BF16)
    proj = jnp.dot(hn, w_ref[...], preferred_element_type=F32)
    q_ref[0] = (proj[:, :SB_WIDTH] * (HEAD_DIM ** -0.5 * LOG2E)).astype(BF16)
    k_ref[0] = proj[:, SB_WIDTH:2 * SB_WIDTH].astype(BF16)
    v_ref[0] = proj[:, 2 * SB_WIDTH:].astype(BF16)


def _odd_in_proj(x, g, w, tm):
    b, l, d = x.shape
    row = pl.BlockSpec((1, tm, SB_WIDTH), lambda i, j: (i, j, 0))
    out = jax.ShapeDtypeStruct((b, l, SB_WIDTH), BF16)
    return pl.pallas_call(
        _odd_in_kernel,
        grid=(b, l // tm),
        in_specs=[pl.BlockSpec((1, tm, d), lambda i, j: (i, j, 0)),
                  _const_spec((1, d)), _const_spec((d, 3 * SB_WIDTH))],
        out_specs=[row, row, row],
        out_shape=[out, out, out],
        compiler_params=_params("parallel", "parallel"),
        name="odd_in_proj",
    )(x, g, w)


def _sb_kernel(q_ref, k_ref, v_ref, tri_ref, o_ref, c_ref, acc_ref, *, tq, tk):
    qi = pl.program_id(2)
    q = q_ref[0]
    n_diag = tq // tk
    n_blocks = (qi + 1) * n_diag
    n_rest = n_blocks - n_diag
    lo_k = lax.broadcasted_iota(jnp.int32, (tk, LANES), 1) < HEAD_DIM
    lo_q = lax.broadcasted_iota(jnp.int32, (tq, LANES), 1) < HEAD_DIM
    c_ref[...] = jnp.zeros_like(c_ref)
    acc_ref[...] = jnp.zeros_like(acc_ref)
    tri2 = tri_ref[...]
    sign = jnp.uint32(0x80000000)

    def visit(j, masked):
        k0 = pl.multiple_of(j * tk, tk)
        kb = k_ref[0, pl.ds(k0, tk), :]
        vb = v_ref[0, pl.ds(k0, tk), :]
        if masked:
            row = qi * tq + lax.broadcasted_iota(jnp.int32, (tq, tk), 0)
            col = k0 + lax.broadcasted_iota(jnp.int32, (tq, tk), 1)
            valid = col < row
        for h in range(2):
            kh = jnp.where(lo_k, kb, 0) if h == 0 else jnp.where(lo_k, 0, kb)
            z = lax.dot_general(q, kh.astype(BF16), (((1,), (1,)), ((), ())),
                                preferred_element_type=F32)
            neg_abs = lax.bitcast_convert_type(lax.bitcast_convert_type(z, jnp.uint32) | sign, F32)
            sp = jnp.maximum(z, 0.0) + jnp.log(1.0 + jnp.exp2(neg_abs)) * LOG2E
            if masked:
                sp = jnp.where(valid, sp, 0.0)
            sp_hi = sp.astype(BF16)
            sp_lo = (sp - sp_hi.astype(F32)).astype(BF16)
            w = jnp.dot(jnp.concatenate([sp_hi, sp_lo], axis=1), tri2, preferred_element_type=F32)
            p = jnp.exp2(z - w)
            if masked:
                p = jnp.where(valid, p, 0.0)
            pv = jnp.dot(p.astype(BF16), vb, preferred_element_type=F32)
            c = c_ref[h]
            acc_ref[h] += pv * jnp.exp2(-c)
            c_ref[h] = c + jnp.broadcast_to(w[:, 0:1], (tq, LANES))

    for d in range(n_diag):
        visit(n_blocks - 1 - d, True)

    def cond(carry):
        i, c_min = carry
        return jnp.logical_and(i < n_rest, c_min < SB_DEAD_LOG2)

    def body(carry):
        i, _ = carry
        visit(n_rest - 1 - i, False)
        return i + 1, jnp.min(c_ref[...])

    lax.while_loop(cond, body, (jnp.int32(0), jnp.min(c_ref[...])))
    o_ref[0] = jnp.where(lo_q, acc_ref[0], acc_ref[1]).astype(BF16)


def _sb_attention(q, k, v, tq, tk):
    b, l, width = q.shape
    tri = (jnp.arange(tk)[:, None] >= jnp.arange(tk)[None, :]).astype(BF16)
    tri = jnp.concatenate([tri, tri], axis=0)
    return pl.pallas_call(
        functools.partial(_sb_kernel, tq=tq, tk=tk),
        grid=(b, width // LANES, l // tq),
        in_specs=[
            pl.BlockSpec((1, tq, LANES), lambda i, h, j: (i, j, h)),
            pl.BlockSpec((1, l, LANES), lambda i, h, j: (i, 0, h)),
            pl.BlockSpec((1, l, LANES), lambda i, h, j: (i, 0, h)),
            _const_spec((2 * tk, tk)),
        ],
        out_specs=pl.BlockSpec((1, tq, LANES), lambda i, h, j: (i, j, h)),
        out_shape=jax.ShapeDtypeStruct((b, l, width), BF16),
        scratch_shapes=[pltpu.VMEM((2, tq, LANES), F32), pltpu.VMEM((2, tq, LANES), F32)],
        compiler_params=_params("parallel", "parallel", "arbitrary"),
        name="sb_attention",
    )(q, k, v, tri)


def _post_kernel(*refs, n_mix):
    x_ref = refs[0]
    mix_refs = refs[1:1 + n_mix]
    wo_refs = refs[1 + n_mix:1 + 2 * n_mix]
    g_ref, wg_ref, wu_ref, wd_ref, o_ref = refs[1 + 2 * n_mix:]
    x = x_ref[0]
    for m_ref, w_ref in zip(mix_refs, wo_refs):
        m = m_ref[0] if len(m_ref.shape) == 3 else m_ref[...]
        x = x + jnp.dot(m, w_ref[...], preferred_element_type=F32)
    hn = _rmsnorm(x, g_ref[...]).astype(BF16)
    gate = jnp.dot(hn, wg_ref[...], preferred_element_type=F32)
    up = jnp.dot(hn, wu_ref[...], preferred_element_type=F32)
    act = (gate * jax.nn.sigmoid(gate) * up).astype(BF16)
    o_ref[0] = x + jnp.dot(act, wd_ref[...], preferred_element_type=F32)


def _post(x, mixes, mix_specs, w_outs, g, w_gate, w_up, w_down, tm):
    b, l, d = x.shape
    xspec = pl.BlockSpec((1, tm, d), lambda i, j: (i, j, 0))
    return pl.pallas_call(
        functools.partial(_post_kernel, n_mix=len(mixes)),
        grid=(b, l // tm),
        in_specs=[xspec, *mix_specs, *[_const_spec(w.shape) for w in w_outs],
                  _const_spec((1, d)), _const_spec(w_gate.shape), _const_spec(w_up.shape),
                  _const_spec(w_down.shape)],
        out_specs=xspec,
        out_shape=jax.ShapeDtypeStruct(x.shape, F32),
        compiler_params=_params("parallel", "parallel"),
        name="out_proj_ffn",
    )(x, *mixes, *w_outs, g, w_gate, w_up, w_down)


def _tiles(l):
    tm = min(512, l)
    tq = min(256, l)
    tk = min(256, tq)
    steps = min(64, l)
    return tm, tq, tk, steps


def kernel(x, even_norm, even_w_in, q_norm, k_norm, sinks, ssm_a_re, ssm_a_im, ssm_b_re, ssm_b_im,
           ssm_c_re, ssm_c_im, ssm_d, ssm_log_dt, ssm_w_glu, ssm_b_glu, even_w_out, odd_norm,
           odd_w_in, odd_w_out, ffn_norm, ffn_w_gate, ffn_w_up, ffn_w_down):
    b, l, d = x.shape
    assert d == D_MODEL and l % SWA_BLOCK == 0
    tm, tq, tk, steps = _tiles(l)
    depth = ffn_norm.shape[0]
    bf = lambda t: t.astype(BF16)
    row = lambda t: t.reshape(1, -1)
    for layer in range(depth):
        i = layer // 2
        if layer % 2 == 0:
            q, k, v, u_t = _even_in_proj(
                x, row(even_norm[i]), bf(even_w_in[i]),
                row(jnp.tile(q_norm[i], 2)), row(jnp.tile(k_norm[i], 2)), tm)
            o_attn = _swa_attention(sinks[i], q, k, v)
            mats = _ssm_prepare(ssm_a_re[i], ssm_a_im[i], ssm_b_re[i], ssm_b_im[i],
                                ssm_c_re[i], ssm_c_im[i], ssm_log_dt[i], b)
            o_ssm = _ssm_mixer(u_t.reshape(l * b, SSM_WIDTH), mats, row(ssm_d[i]),
                               bf(ssm_w_glu[i]), row(ssm_b_glu[i]), b, steps)
            mixes = [o_attn, o_ssm.reshape(l, b * SSM_WIDTH)]
            mix_specs = [pl.BlockSpec((1, tm, SWA_WIDTH), lambda bi, j: (bi, j, 0)),
                         pl.BlockSpec((tm, SSM_WIDTH), lambda bi, j: (j, bi))]
            w_outs = [bf(even_w_out[i][:SWA_WIDTH]), bf(even_w_out[i][SWA_WIDTH:])]
        else:
            q, k, v = _odd_in_proj(x, row(odd_norm[i]), bf(odd_w_in[i]), tm)
            mixes = [_sb_attention(q, k, v, tq, tk)]
            mix_specs = [pl.BlockSpec((1, tm, SB_WIDTH), lambda bi, j: (bi, j, 0))]
            w_outs = [bf(odd_w_out[i])]
        x = _post(x, mixes, mix_specs, w_outs, row(ffn_norm[layer]), bf(ffn_w_gate[layer]),
                  bf(ffn_w_up[layer]), bf(ffn_w_down[layer]), tm)
    return x
```

```python
import functools
import math

import jax
import jax.numpy as jnp
from jax import lax
from jax.experimental import pallas as pl
from jax.experimental.pallas import tpu as pltpu

F32 = jnp.float32
BF16 = jnp.bfloat16

D_MODEL = 1024
HEAD_DIM = 64
EPS = 1e-6
LANES = 128

SWA_Q_HEADS = 8
SWA_KV_HEADS = 2
SWA_BLOCK = 128
SWA_WIDTH = SWA_Q_HEADS * HEAD_DIM
KV_WIDTH = SWA_KV_HEADS * HEAD_DIM

SSM_WIDTH = D_MODEL // 2
SSM_GROUP_CH = 16
SSM_GROUPS = SSM_WIDTH // SSM_GROUP_CH
SSM_STATE = 64
SSM_STATES = SSM_GROUPS * SSM_STATE
SSM_CHUNKS = SSM_WIDTH // LANES
SSM_CHUNK_STATES = SSM_STATES // SSM_CHUNKS
EVEN_IN = SWA_WIDTH + 2 * KV_WIDTH + SSM_WIDTH

SB_HEADS = D_MODEL // HEAD_DIM
SB_WIDTH = SB_HEADS * HEAD_DIM
LOG2E = math.log2(math.e)
SB_DEAD_LOG2 = 160.0

VMEM_LIMIT_BYTES = 56 * 1024 * 1024


def _params(*sem):
    return pltpu.CompilerParams(dimension_semantics=sem, vmem_limit_bytes=VMEM_LIMIT_BYTES)


def _const_spec(shape):
    nd = len(shape)
    return pl.BlockSpec(shape, lambda *_: (0,) * nd, pipeline_mode=pl.Buffered(1))


def _rmsnorm(x, g):
    ms = jnp.mean(x * x, axis=-1, keepdims=True)
    return x * lax.rsqrt(ms + EPS) * g


def _pair_rmsnorm(x, g2):
    lo = lax.broadcasted_iota(jnp.int32, x.shape, 1) < HEAD_DIM
    sq = x * x
    s_lo = jnp.sum(jnp.where(lo, sq, 0.0), axis=-1, keepdims=True)
    s_hi = jnp.sum(jnp.where(lo, 0.0, sq), axis=-1, keepdims=True)
    ms = jnp.where(lo, s_lo, s_hi) * (1.0 / HEAD_DIM)
    return x * lax.rsqrt(ms + EPS) * g2


def _even_in_kernel(x_ref, g_ref, w_ref, qg_ref, kg_ref, q_ref, k_ref, v_ref, u_ref):
    hn = _rmsnorm(x_ref[0], g_ref[...]).astype(BF16)
    proj = jnp.dot(hn, w_ref[...], preferred_element_type=F32)
    scale = HEAD_DIM ** -0.5
    for p in range(SWA_WIDTH // LANES):
        qp = _pair_rmsnorm(proj[:, p * LANES:(p + 1) * LANES], qg_ref[...])
        q_ref[0, :, p * LANES:(p + 1) * LANES] = (qp * scale).astype(BF16)
    k = _pair_rmsnorm(proj[:, SWA_WIDTH:SWA_WIDTH + KV_WIDTH], kg_ref[...])
    k_ref[0] = k.astype(BF16)
    v_ref[0] = proj[:, SWA_WIDTH + KV_WIDTH:SWA_WIDTH + 2 * KV_WIDTH].astype(BF16)
    u_ref[...] = proj[:, SWA_WIDTH + 2 * KV_WIDTH:]


def _even_in_proj(x, g, w, qg, kg, tm):
    b, l, d = x.shape
    return pl.pallas_call(
        _even_in_kernel,
        grid=(b, l // tm),
        in_specs=[
            pl.BlockSpec((1, tm, d), lambda i, j: (i, j, 0)),
            _const_spec((1, d)),
            _const_spec((d, EVEN_IN)),
            _const_spec((1, LANES)),
            _const_spec((1, LANES)),
        ],
        out_specs=[
            pl.BlockSpec((1, tm, SWA_WIDTH), lambda i, j: (i, j, 0)),
            pl.BlockSpec((1, tm, KV_WIDTH), lambda i, j: (i, j, 0)),
            pl.BlockSpec((1, tm, KV_WIDTH), lambda i, j: (i, j, 0)),
            pl.BlockSpec((tm, SSM_WIDTH), lambda i, j: (j, i)),
        ],
        out_shape=[
            jax.ShapeDtypeStruct((b, l, SWA_WIDTH), BF16),
            jax.ShapeDtypeStruct((b, l, KV_WIDTH), BF16),
            jax.ShapeDtypeStruct((b, l, KV_WIDTH), BF16),
            jax.ShapeDtypeStruct((l, b * SSM_WIDTH), F32),
        ],
        compiler_params=_params("parallel", "parallel"),
        name="even_in_proj",
    )(x, g, w, qg, kg)


def _swa_kernel(sink_ref, q_ref, kc_ref, kp_ref, vc_ref, vp_ref, o_ref):
    n = pl.program_id(1)
    blk = SWA_BLOCK
    kcat = jnp.concatenate([kp_ref[0], kc_ref[0]], axis=0).astype(F32)
    vcat = jnp.concatenate([vp_ref[0], vc_ref[0]], axis=0).astype(F32)
    lo = lax.broadcasted_iota(jnp.int32, kcat.shape, 1) < HEAD_DIM

    def halves(t):
        g0_lo = jnp.where(lo, t, 0.0)
        g1_hi = jnp.where(lo, 0.0, t)
        g0_hi = pltpu.roll(g0_lo, HEAD_DIM, 1)
        g1_lo = pltpu.roll(g1_hi, HEAD_DIM, 1)
        return [[g0_lo.astype(BF16), g0_hi.astype(BF16)], [g1_lo.astype(BF16), g1_hi.astype(BF16)]]

    ks, vs = halves(kcat), halves(vcat)
    qq = lax.broadcasted_iota(jnp.int32, (blk, 2 * blk), 0)
    kk = lax.broadcasted_iota(jnp.int32, (blk, 2 * blk), 1)
    diff = qq + blk - kk
    first_key = jnp.where(n > 0, 0, blk)
    mask = (diff >= 0) & (diff < blk) & (kk >= first_key)
    for p in range(SWA_WIDTH // LANES):
        qp = q_ref[0, :, p * LANES:(p + 1) * LANES]
        g = (2 * p) // (SWA_Q_HEADS // SWA_KV_HEADS)
        acc = None
        for e in range(2):
            sink = sink_ref[2 * p + e]
            z = lax.dot_general(qp, ks[g][e], (((1,), (1,)), ((), ())), preferred_element_type=F32)
            s = jnp.where(mask, z, -jnp.inf)
            m = jnp.maximum(jnp.max(s, axis=-1, keepdims=True), sink)
            pe = jnp.exp(s - m)
            den = jnp.sum(pe, axis=-1, keepdims=True) + jnp.exp(sink - m)
            o = jnp.dot(pe.astype(BF16), vs[g][e], preferred_element_type=F32) / den
            acc = o if acc is None else acc + o
        o_ref[0, :, p * LANES:(p + 1) * LANES] = acc.astype(BF16)


def _swa_attention(sinks, q, k, v):
    b, l, _ = q.shape
    blk = SWA_BLOCK
    cur = lambda i, j: (i, j, 0)
    prev = lambda i, j: (i, jnp.maximum(j - 1, 0), 0)
    return pl.pallas_call(
        _swa_kernel,
        grid=(b, l // blk),
        in_specs=[
            pl.BlockSpec(memory_space=pltpu.SMEM),
            pl.BlockSpec((1, blk, SWA_WIDTH), cur),
            pl.BlockSpec((1, blk, KV_WIDTH), cur),
            pl.BlockSpec((1, blk, KV_WIDTH), prev),
            pl.BlockSpec((1, blk, KV_WIDTH), cur),
            pl.BlockSpec((1, blk, KV_WIDTH), prev),
        ],
        out_specs=pl.BlockSpec((1, blk, SWA_WIDTH), cur),
        out_shape=jax.ShapeDtypeStruct((b, l, SWA_WIDTH), BF16),
        compiler_params=_params("parallel", "parallel"),
        name="swa_attention",
    )(sinks, q, k, k, v, v)


def _ssm_disc_kernel(are_ref, aim_ref, ldt_ref, lre_ref, lim_ref, wre_ref, wim_ref):
    a_re, a_im = are_ref[...], aim_ref[...]
    dt = jnp.exp(ldt_ref[...])
    mag = jnp.exp(a_re * dt)
    lam_re = mag * jnp.cos(a_im * dt)
    lam_im = mag * jnp.sin(a_im * dt)
    den = a_re * a_re + a_im * a_im
    lre_ref[...] = lam_re
    lim_ref[...] = lam_im
    wre_ref[...] = ((lam_re - 1.0) * a_re + lam_im * a_im) / den
    wim_ref[...] = (lam_im * a_re - (lam_re - 1.0) * a_im) / den


def _ssm_bbar_kernel(wre_ref, wim_ref, bre_ref, bim_ref, ore_ref, oim_ref):
    w_re, w_im, b_re, b_im = wre_ref[...], wim_ref[...], bre_ref[...], bim_ref[...]
    ore_ref[...] = w_re * b_re - w_im * b_im
    oim_ref[...] = w_re * b_im + w_im * b_re


def _block_diag(t):
    ch, gl, r, c = t.shape
    eye = jnp.eye(gl, dtype=t.dtype)
    return (t[:, :, :, None, :] * eye[None, :, None, :, None]).reshape(ch, gl * r, gl * c)


def _ssm_prepare(a_re, a_im, b_re, b_im, c_re, c_im, log_dt, batch):
    g, n, p = SSM_GROUPS, SSM_STATE, SSM_GROUP_CH
    gn = jax.ShapeDtypeStruct((g, n), F32)
    ldt = jnp.broadcast_to(log_dt[:, None], (g, n))
    lam_re, lam_im, w_re, w_im = pl.pallas_call(
        _ssm_disc_kernel, out_shape=[gn, gn, gn, gn], name="ssm_discretise")(a_re, a_im, ldt)
    gnp = jax.ShapeDtypeStruct((g, n * p), F32)
    bb_re, bb_im = pl.pallas_call(_ssm_bbar_kernel, out_shape=[gnp, gnp], name="ssm_bbar")(
        jnp.repeat(w_re, p, axis=1), jnp.repeat(w_im, p, axis=1),
        b_re.reshape(g, n * p), b_im.reshape(g, n * p))
    gl = g // SSM_CHUNKS
    to_in = lambda t: _block_diag(
        t.reshape(SSM_CHUNKS, gl, n, p).transpose(0, 1, 3, 2)).astype(BF16)
    to_out = lambda t: _block_diag(
        t.reshape(SSM_CHUNKS, gl, p, n).transpose(0, 1, 3, 2)).astype(BF16)
    bcast = lambda t: jnp.broadcast_to(t.reshape(1, SSM_STATES), (batch, SSM_STATES))
    return (to_in(bb_re), to_in(bb_im), bcast(lam_re), bcast(lam_im), to_out(c_re), to_out(c_im))


def _ssm_kernel(u_ref, bre_ref, bim_ref, lre_ref, lim_ref, cre_ref, cim_ref, d_ref, wg_ref, bg_ref,
                o_ref, hre_ref, him_ref, sre_ref, sim_ref, *, batch, steps):
    @pl.when(pl.program_id(0) == 0)
    def _():
        sre_ref[...] = jnp.zeros_like(sre_ref)
        sim_ref[...] = jnp.zeros_like(sim_ref)

    u = u_ref[...]
    ub = u.astype(BF16)
    ys = []
    for c in range(SSM_CHUNKS):
        cs = slice(c * SSM_CHUNK_STATES, (c + 1) * SSM_CHUNK_STATES)
        uc = ub[:, c * LANES:(c + 1) * LANES]
        hre_ref[:, cs] = jnp.dot(uc, bre_ref[c], preferred_element_type=F32)
        him_ref[:, cs] = jnp.dot(uc, bim_ref[c], preferred_element_type=F32)
        h_re, h_im = sre_ref[:, cs], sim_ref[:, cs]
        l_re, l_im = lre_ref[:, cs], lim_ref[:, cs]
        for t in range(steps):
            rows = slice(t * batch, (t + 1) * batch)
            n_re = l_re * h_re - l_im * h_im + hre_ref[rows, cs]
            n_im = l_re * h_im + l_im * h_re + him_ref[rows, cs]
            hre_ref[rows, cs] = n_re
            him_ref[rows, cs] = n_im
            h_re, h_im = n_re, n_im
        sre_ref[:, cs] = h_re
        sim_ref[:, cs] = h_im
        ys.append(jnp.dot(hre_ref[:, cs].astype(BF16), cre_ref[c], preferred_element_type=F32)
                  - jnp.dot(him_ref[:, cs].astype(BF16), cim_ref[c], preferred_element_type=F32))
    y = jnp.concatenate(ys, axis=-1) + d_ref[...] * u
    y = jax.nn.gelu(y)
    gate = jnp.dot(y.astype(BF16), wg_ref[...], preferred_element_type=F32) + bg_ref[...]
    o_ref[...] = (y * jax.nn.sigmoid(gate)).astype(BF16)


def _ssm_mixer(u_tb, mats, d_skip, w_glu, b_glu, batch, steps):
    rows_total, width = u_tb.shape
    rows = steps * batch
    b_in_re, b_in_im, lam_re, lam_im, c_out_re, c_out_im = mats
    return pl.pallas_call(
        functools.partial(_ssm_kernel, batch=batch, steps=steps),
        grid=(rows_total // rows,),
        in_specs=[
            pl.BlockSpec((rows, width), lambda i: (i, 0)),
            _const_spec(b_in_re.shape), _const_spec(b_in_im.shape),
            _const_spec(lam_re.shape), _const_spec(lam_im.shape),
            _const_spec(c_out_re.shape), _const_spec(c_out_im.shape),
            _const_spec((1, width)), _const_spec((width, width)), _const_spec((1, width)),
        ],
        out_specs=pl.BlockSpec((rows, width), lambda i: (i, 0)),
        out_shape=jax.ShapeDtypeStruct((rows_total, width), BF16),
        scratch_shapes=[
            pltpu.VMEM((rows, SSM_STATES), F32), pltpu.VMEM((rows, SSM_STATES), F32),
            pltpu.VMEM((batch, SSM_STATES), F32), pltpu.VMEM((batch, SSM_STATES), F32),
        ],
        compiler_params=_params("arbitrary"),
        name="ssm_mixer",
    )(u_tb, b_in_re, b_in_im, lam_re, lam_im, c_out_re, c_out_im, d_skip, w_glu, b_glu)


def _odd_in_kernel(x_ref, g_ref, w_ref, q_ref, k_ref, v_ref):
    hn = _rmsnorm(x_ref[0], g_ref[...]).astype(BF16)
    proj = jnp.dot(hn, w_ref[...], preferred_element_type=F32)
    q_ref[0] = (proj[:, :SB_WIDTH] * (HEAD_DIM ** -0.5 * LOG2E)).astype(BF16)
    k_ref[0] = proj[:, SB_WIDTH:2 * SB_WIDTH].astype(BF16)
    v_ref[0] = proj[:, 2 * SB_WIDTH:].astype(BF16)


def _odd_in_proj(x, g, w, tm):
    b, l, d = x.shape
    row = pl.BlockSpec((1, tm, SB_WIDTH), lambda i, j: (i, j, 0))
    out = jax.ShapeDtypeStruct((b, l, SB_WIDTH), BF16)
    return pl.pallas_call(
        _odd_in_kernel,
        grid=(b, l // tm),
        in_specs=[pl.BlockSpec((1, tm, d), lambda i, j: (i, j, 0)),
                  _const_spec((1, d)), _const_spec((d, 3 * SB_WIDTH))],
        out_specs=[row, row, row],
        out_shape=[out, out, out],
        compiler_params=_params("parallel", "parallel"),
        name="odd_in_proj",
    )(x, g, w)


def _sb_stream(q, kh, vb, tri, valid):
    z = lax.dot_general(q, kh, (((1,), (1,)), ((), ())), preferred_element_type=F32)
    sp = jnp.maximum(z, 0.0) + jnp.log(1.0 + jnp.exp2(-jnp.abs(z))) * LOG2E
    if valid is not None:
        sp = jnp.where(valid, sp, 0.0)
    w = jnp.dot(sp.astype(BF16), tri, preferred_element_type=F32)
    p = jnp.exp2(z - w)
    if valid is not None:
        p = jnp.where(valid, p, 0.0)
    pv = jnp.dot(p.astype(BF16), vb, preferred_element_type=F32)
    return pv, jnp.broadcast_to(w[:, 0:1], (q.shape[0], LANES))


def _sb_kernel(q_ref, k_ref, v_ref, tri_ref, o_ref, c_ref, acc_ref, *, th):
    qi = pl.program_id(2)
    lo_k = lax.broadcasted_iota(jnp.int32, (th, LANES), 1) < HEAD_DIM
    lo_q = lax.broadcasted_iota(jnp.int32, (th, LANES), 1) < HEAD_DIM
    tri = tri_ref[...]
    strict = (lax.broadcasted_iota(jnp.int32, (th, th), 1)
              < lax.broadcasted_iota(jnp.int32, (th, th), 0))

    def load_kv(j):
        rows = pl.ds(pl.multiple_of(j * th, th), th)
        kb = k_ref[0, rows, :]
        return (jnp.where(lo_k, kb, 0), jnp.where(lo_k, 0, kb)), v_ref[0, rows, :]

    j_diag = (2 * qi, 2 * qi + 1)
    kv_left = load_kv(jnp.maximum(2 * qi - 1, 0))
    kv_diag = (load_kv(j_diag[0]), load_kv(j_diag[1]))
    has_left = (qi > 0).astype(F32)

    for half in range(2):
        q = q_ref[0, half * th:(half + 1) * th, :]
        (kd, vd) = kv_diag[half]
        (kl, vl) = kv_diag[0] if half == 1 else kv_left
        for h in range(2):
            pv_d, c_d = _sb_stream(q, kd[h], vd, tri, strict)
            pv_l, c_l = _sb_stream(q, kl[h], vl, tri, None)
            scale = jnp.exp2(-c_d)
            if half == 0:
                scale, c_l = scale * has_left, c_l * has_left
            acc_ref[half, h] = pv_d + pv_l * scale
            c_ref[half, h] = c_d + c_l

    for half in range(2):
        q = q_ref[0, half * th:(half + 1) * th, :]
        n_rest = jnp.maximum(j_diag[half] - 1, 0)

        def cond(carry):
            i, c_min = carry
            return jnp.logical_and(i < n_rest, c_min < SB_DEAD_LOG2)

        def body(carry, q=q, half=half, n_rest=n_rest):
            i, _ = carry
            ks, vb = load_kv(n_rest - 1 - i)
            for h in range(2):
                pv, c_blk = _sb_stream(q, ks[h], vb, tri, None)
                c = c_ref[half, h]
                acc_ref[half, h] += pv * jnp.exp2(-c)
                c_ref[half, h] = c + c_blk
            return i + 1, jnp.min(c_ref[half])

        lax.while_loop(cond, body, (jnp.int32(0), jnp.min(c_ref[half])))
        o_ref[0, half * th:(half + 1) * th, :] = jnp.where(
            lo_q, acc_ref[half, 0], acc_ref[half, 1]).astype(BF16)


def _sb_attention(q, k, v, th):
    b, l, width = q.shape
    tri = (jnp.arange(th)[:, None] >= jnp.arange(th)[None, :]).astype(BF16)
    tq = 2 * th
    return pl.pallas_call(
        functools.partial(_sb_kernel, th=th),
        grid=(b, width // LANES, l // tq),
        in_specs=[
            pl.BlockSpec((1, tq, LANES), lambda i, h, j: (i, j, h)),
            pl.BlockSpec((1, l, LANES), lambda i, h, j: (i, 0, h)),
            pl.BlockSpec((1, l, LANES), lambda i, h, j: (i, 0, h)),
            _const_spec((th, th)),
        ],
        out_specs=pl.BlockSpec((1, tq, LANES), lambda i, h, j: (i, j, h)),
        out_shape=jax.ShapeDtypeStruct((b, l, width), BF16),
        scratch_shapes=[pltpu.VMEM((2, 2, th, LANES), F32), pltpu.VMEM((2, 2, th, LANES), F32)],
        compiler_params=_params("parallel", "parallel", "arbitrary"),
        name="sb_attention",
    )(q, k, v, tri)


def _post_kernel(*refs, n_mix):
    x_ref = refs[0]
    mix_refs = refs[1:1 + n_mix]
    wo_refs = refs[1 + n_mix:1 + 2 * n_mix]
    g_ref, wg_ref, wu_ref, wd_ref, o_ref = refs[1 + 2 * n_mix:]
    x = x_ref[0]
    for m_ref, w_ref in zip(mix_refs, wo_refs):
        m = m_ref[0] if len(m_ref.shape) == 3 else m_ref[...]
        x = x + jnp.dot(m, w_ref[...], preferred_element_type=F32)
    hn = _rmsnorm(x, g_ref[...]).astype(BF16)
    gate = jnp.dot(hn, wg_ref[...], preferred_element_type=F32)
    up = jnp.dot(hn, wu_ref[...], preferred_element_type=F32)
    act = (gate * jax.nn.sigmoid(gate) * up).astype(BF16)
    o_ref[0] = x + jnp.dot(act, wd_ref[...], preferred_element_type=F32)


def _post(x, mixes, mix_specs, w_outs, g, w_gate, w_up, w_down, tm):
    b, l, d = x.shape
    xspec = pl.BlockSpec((1, tm, d), lambda i, j: (i, j, 0))
    return pl.pallas_call(
        functools.partial(_post_kernel, n_mix=len(mixes)),
        grid=(b, l // tm),
        in_specs=[xspec, *mix_specs, *[_const_spec(w.shape) for w in w_outs],
                  _const_spec((1, d)), _const_spec(w_gate.shape), _const_spec(w_up.shape),
                  _const_spec(w_down.shape)],
        out_specs=xspec,
        out_shape=jax.ShapeDtypeStruct(x.shape, F32),
        compiler_params=_params("parallel", "parallel"),
        name="out_proj_ffn",
    )(x, *mixes, *w_outs, g, w_gate, w_up, w_down)


def _tiles(l):
    tm = min(512, l)
    th = min(256, l // 2)
    steps = min(64, l)
    return tm, th, steps


def kernel(x, even_norm, even_w_in, q_norm, k_norm, sinks, ssm_a_re, ssm_a_im, ssm_b_re, ssm_b_im,
           ssm_c_re, ssm_c_im, ssm_d, ssm_log_dt, ssm_w_glu, ssm_b_glu, even_w_out, odd_norm,
           odd_w_in, odd_w_out, ffn_norm, ffn_w_gate, ffn_w_up, ffn_w_down):
    b, l, d = x.shape
    assert d == D_MODEL and l % SWA_BLOCK == 0
    tm, th, steps = _tiles(l)
    depth = ffn_norm.shape[0]
    bf = lambda t: t.astype(BF16)
    row = lambda t: t.reshape(1, -1)
    for layer in range(depth):
        i = layer // 2
        if layer % 2 == 0:
            q, k, v, u_t = _even_in_proj(
                x, row(even_norm[i]), bf(even_w_in[i]),
                row(jnp.tile(q_norm[i], 2)), row(jnp.tile(k_norm[i], 2)), tm)
            o_attn = _swa_attention(sinks[i], q, k, v)
            mats = _ssm_prepare(ssm_a_re[i], ssm_a_im[i], ssm_b_re[i], ssm_b_im[i],
                                ssm_c_re[i], ssm_c_im[i], ssm_log_dt[i], b)
            o_ssm = _ssm_mixer(u_t.reshape(l * b, SSM_WIDTH), mats, row(ssm_d[i]),
                               bf(ssm_w_glu[i]), row(ssm_b_glu[i]), b, steps)
            mixes = [o_attn, o_ssm.reshape(l, b * SSM_WIDTH)]
            mix_specs = [pl.BlockSpec((1, tm, SWA_WIDTH), lambda bi, j: (bi, j, 0)),
                         pl.BlockSpec((tm, SSM_WIDTH), lambda bi, j: (j, bi))]
            w_outs = [bf(even_w_out[i][:SWA_WIDTH]), bf(even_w_out[i][SWA_WIDTH:])]
        else:
            q, k, v = _odd_in_proj(x, row(odd_norm[i]), bf(odd_w_in[i]), tm)
            mixes = [_sb_attention(q, k, v, th)]
            mix_specs = [pl.BlockSpec((1, tm, SB_WIDTH), lambda bi, j: (bi, j, 0))]
            w_outs = [bf(odd_w_out[i])]
        x = _post(x, mixes, mix_specs, w_outs, row(ffn_norm[layer]), bf(ffn_w_gate[layer]),
                  bf(ffn_w_up[layer]), bf(ffn_w_down[layer]), tm)
    return x
```

```python
import functools
import math

import jax
import jax.numpy as jnp
from jax import lax
from jax.experimental import pallas as pl
from jax.experimental.pallas import tpu as pltpu

F32 = jnp.float32
BF16 = jnp.bfloat16

D_MODEL = 1024
HEAD_DIM = 64
EPS = 1e-6
LANES = 128

SWA_Q_HEADS = 8
SWA_KV_HEADS = 2
SWA_BLOCK = 128
SWA_WIDTH = SWA_Q_HEADS * HEAD_DIM
KV_WIDTH = SWA_KV_HEADS * HEAD_DIM

SSM_WIDTH = D_MODEL // 2
SSM_GROUP_CH = 16
SSM_GROUPS = SSM_WIDTH // SSM_GROUP_CH
SSM_STATE = 64
SSM_STATES = SSM_GROUPS * SSM_STATE
SSM_CHUNKS = SSM_WIDTH // LANES
SSM_CHUNK_STATES = SSM_STATES // SSM_CHUNKS
EVEN_IN = SWA_WIDTH + 2 * KV_WIDTH + SSM_WIDTH

SB_HEADS = D_MODEL // HEAD_DIM
SB_WIDTH = SB_HEADS * HEAD_DIM
LOG2E = math.log2(math.e)
SB_DEAD_LOG2 = 160.0

VMEM_LIMIT_BYTES = 56 * 1024 * 1024


def _params(*sem):
    return pltpu.CompilerParams(dimension_semantics=sem, vmem_limit_bytes=VMEM_LIMIT_BYTES)


def _const_spec(shape):
    nd = len(shape)
    return pl.BlockSpec(shape, lambda *_: (0,) * nd, pipeline_mode=pl.Buffered(1))


def _rmsnorm(x, g):
    ms = jnp.mean(x * x, axis=-1, keepdims=True)
    return x * lax.rsqrt(ms + EPS) * g


def _pair_rmsnorm(x, g2):
    lo = lax.broadcasted_iota(jnp.int32, x.shape, 1) < HEAD_DIM
    sq = x * x
    s_lo = jnp.sum(jnp.where(lo, sq, 0.0), axis=-1, keepdims=True)
    s_hi = jnp.sum(jnp.where(lo, 0.0, sq), axis=-1, keepdims=True)
    ms = jnp.where(lo, s_lo, s_hi) * (1.0 / HEAD_DIM)
    return x * lax.rsqrt(ms + EPS) * g2


def _even_in_kernel(x_ref, g_ref, w_ref, qg_ref, kg_ref, q_ref, k_ref, v_ref, u_ref):
    hn = _rmsnorm(x_ref[0], g_ref[...]).astype(BF16)
    proj = jnp.dot(hn, w_ref[...], preferred_element_type=F32)
    scale = HEAD_DIM ** -0.5 * LOG2E
    for p in range(SWA_WIDTH // LANES):
        qp = _pair_rmsnorm(proj[:, p * LANES:(p + 1) * LANES], qg_ref[...])
        q_ref[0, :, p * LANES:(p + 1) * LANES] = (qp * scale).astype(BF16)
    k = _pair_rmsnorm(proj[:, SWA_WIDTH:SWA_WIDTH + KV_WIDTH], kg_ref[...])
    k_ref[0] = k.astype(BF16)
    v_ref[0] = proj[:, SWA_WIDTH + KV_WIDTH:SWA_WIDTH + 2 * KV_WIDTH].astype(BF16)
    u_ref[...] = proj[:, SWA_WIDTH + 2 * KV_WIDTH:]


def _even_in_proj(x, g, w, qg, kg, tm):
    b, l, d = x.shape
    return pl.pallas_call(
        _even_in_kernel,
        grid=(b, l // tm),
        in_specs=[
            pl.BlockSpec((1, tm, d), lambda i, j: (i, j, 0)),
            _const_spec((1, d)),
            _const_spec((d, EVEN_IN)),
            _const_spec((1, LANES)),
            _const_spec((1, LANES)),
        ],
        out_specs=[
            pl.BlockSpec((1, tm, SWA_WIDTH), lambda i, j: (i, j, 0)),
            pl.BlockSpec((1, tm, KV_WIDTH), lambda i, j: (i, j, 0)),
            pl.BlockSpec((1, tm, KV_WIDTH), lambda i, j: (i, j, 0)),
            pl.BlockSpec((tm, SSM_WIDTH), lambda i, j: (j, i)),
        ],
        out_shape=[
            jax.ShapeDtypeStruct((b, l, SWA_WIDTH), BF16),
            jax.ShapeDtypeStruct((b, l, KV_WIDTH), BF16),
            jax.ShapeDtypeStruct((b, l, KV_WIDTH), BF16),
            jax.ShapeDtypeStruct((l, b * SSM_WIDTH), F32),
        ],
        compiler_params=_params("parallel", "parallel"),
        name="even_in_proj",
    )(x, g, w, qg, kg)


def _swa_kernel(sink_ref, q_ref, kc_ref, kp_ref, vc_ref, vp_ref, o_ref, *, nsub):
    n = pl.program_id(1)
    blk = SWA_BLOCK
    kall = jnp.concatenate([kp_ref[0], kc_ref[0]], axis=0).astype(F32)
    vall = jnp.concatenate([vp_ref[0], vc_ref[0]], axis=0).astype(F32)
    lo = lax.broadcasted_iota(jnp.int32, kall.shape, 1) < HEAD_DIM

    def halves(t):
        g0_lo = jnp.where(lo, t, 0.0)
        g1_hi = jnp.where(lo, 0.0, t)
        g0_hi = pltpu.roll(g0_lo, HEAD_DIM, 1)
        g1_lo = pltpu.roll(g1_hi, HEAD_DIM, 1)
        return [[g0_lo.astype(BF16), g0_hi.astype(BF16)], [g1_lo.astype(BF16), g1_hi.astype(BF16)]]

    ks, vs = halves(kall), halves(vall)
    qq = lax.broadcasted_iota(jnp.int32, (blk, 2 * blk), 0)
    kk = lax.broadcasted_iota(jnp.int32, (blk, 2 * blk), 1)
    diff = qq + blk - kk
    band = (diff >= 0) & (diff < blk)
    first_key = jnp.where(n > 0, 0, blk)
    band_first = band & (kk >= first_key)

    streams = []
    for s in range(nsub):
        keys = slice(s * blk, (s + 2) * blk)
        for p in range(SWA_WIDTH // LANES):
            g = (2 * p) // (SWA_Q_HEADS // SWA_KV_HEADS)
            for e in range(2):
                streams.append(dict(
                    q=q_ref[0, s * blk:(s + 1) * blk, p * LANES:(p + 1) * LANES],
                    k=ks[g][e][keys], v=vs[g][e][keys],
                    sink=sink_ref[2 * p + e] * LOG2E, mask=band_first if s == 0 else band))
    group = len(streams) // nsub
    for g0 in range(0, len(streams), group):
        for st in streams[g0:g0 + group]:
            st["z"] = lax.dot_general(st["q"], st["k"], (((1,), (1,)), ((), ())),
                                      preferred_element_type=F32)
        for st in streams[g0:g0 + group]:
            z = jnp.where(st["mask"], st.pop("z"), -jnp.inf)
            m = jnp.maximum(jnp.max(z, axis=-1, keepdims=True), st["sink"])
            pe = jnp.exp2(z - m)
            st["den"] = jnp.sum(pe, axis=-1, keepdims=True) + jnp.exp2(st["sink"] - m)
            st["pe"] = pe.astype(BF16)
        for st in streams[g0:g0 + group]:
            st["o"] = jnp.dot(st.pop("pe"), st["v"], preferred_element_type=F32) / st["den"]
    for s in range(nsub):
        for p in range(SWA_WIDTH // LANES):
            i = (s * (SWA_WIDTH // LANES) + p) * 2
            o_ref[0, s * blk:(s + 1) * blk, p * LANES:(p + 1) * LANES] = (
                streams[i]["o"] + streams[i + 1]["o"]).astype(BF16)


def _swa_attention(sinks, q, k, v, nsub):
    b, l, _ = q.shape
    blk = SWA_BLOCK * nsub
    cur = lambda i, j: (i, j, 0)
    prev = lambda i, j: (i, jnp.maximum(j * nsub - 1, 0), 0)
    return pl.pallas_call(
        functools.partial(_swa_kernel, nsub=nsub),
        grid=(b, l // blk),
        in_specs=[
            pl.BlockSpec(memory_space=pltpu.SMEM),
            pl.BlockSpec((1, blk, SWA_WIDTH), cur),
            pl.BlockSpec((1, blk, KV_WIDTH), cur),
            pl.BlockSpec((1, SWA_BLOCK, KV_WIDTH), prev),
            pl.BlockSpec((1, blk, KV_WIDTH), cur),
            pl.BlockSpec((1, SWA_BLOCK, KV_WIDTH), prev),
        ],
        out_specs=pl.BlockSpec((1, blk, SWA_WIDTH), cur),
        out_shape=jax.ShapeDtypeStruct((b, l, SWA_WIDTH), BF16),
        compiler_params=_params("parallel", "parallel"),
        name="swa_attention",
    )(sinks, q, k, k, v, v)


def _ssm_disc_kernel(are_ref, aim_ref, ldt_ref, lre_ref, lim_ref, wre_ref, wim_ref):
    a_re, a_im = are_ref[...], aim_ref[...]
    dt = jnp.exp(ldt_ref[...])
    mag = jnp.exp(a_re * dt)
    lam_re = mag * jnp.cos(a_im * dt)
    lam_im = mag * jnp.sin(a_im * dt)
    den = a_re * a_re + a_im * a_im
    lre_ref[...] = lam_re
    lim_ref[...] = lam_im
    wre_ref[...] = ((lam_re - 1.0) * a_re + lam_im * a_im) / den
    wim_ref[...] = (lam_im * a_re - (lam_re - 1.0) * a_im) / den


def _ssm_bbar_kernel(wre_ref, wim_ref, bre_ref, bim_ref, ore_ref, oim_ref):
    w_re, w_im, b_re, b_im = wre_ref[...], wim_ref[...], bre_ref[...], bim_ref[...]
    ore_ref[...] = w_re * b_re - w_im * b_im
    oim_ref[...] = w_re * b_im + w_im * b_re


def _block_diag(t):
    ch, gl, r, c = t.shape
    eye = jnp.eye(gl, dtype=t.dtype)
    return (t[:, :, :, None, :] * eye[None, :, None, :, None]).reshape(ch, gl * r, gl * c)


def _ssm_prepare(a_re, a_im, b_re, b_im, c_re, c_im, log_dt, batch):
    g, n, p = SSM_GROUPS, SSM_STATE, SSM_GROUP_CH
    gn = jax.ShapeDtypeStruct((g, n), F32)
    ldt = jnp.broadcast_to(log_dt[:, None], (g, n))
    lam_re, lam_im, w_re, w_im = pl.pallas_call(
        _ssm_disc_kernel, out_shape=[gn, gn, gn, gn], name="ssm_discretise")(a_re, a_im, ldt)
    gnp = jax.ShapeDtypeStruct((g, n * p), F32)
    bb_re, bb_im = pl.pallas_call(_ssm_bbar_kernel, out_shape=[gnp, gnp], name="ssm_bbar")(
        jnp.repeat(w_re, p, axis=1), jnp.repeat(w_im, p, axis=1),
        b_re.reshape(g, n * p), b_im.reshape(g, n * p))
    gl = g // SSM_CHUNKS
    to_in = lambda t: _block_diag(
        t.reshape(SSM_CHUNKS, gl, n, p).transpose(0, 1, 3, 2)).astype(BF16)
    to_out = lambda t: _block_diag(
        t.reshape(SSM_CHUNKS, gl, p, n).transpose(0, 1, 3, 2)).astype(BF16)
    bcast = lambda t: jnp.broadcast_to(t.reshape(1, SSM_STATES), (batch, SSM_STATES))
    return (to_in(bb_re), to_in(bb_im), bcast(lam_re), bcast(lam_im), to_out(c_re), to_out(c_im))


def _ssm_kernel(u_ref, bre_ref, bim_ref, lre_ref, lim_ref, cre_ref, cim_ref, d_ref, wg_ref, bg_ref,
                o_ref, hre_ref, him_ref, sre_ref, sim_ref, *, batch, steps):
    @pl.when(pl.program_id(0) == 0)
    def _():
        sre_ref[...] = jnp.zeros_like(sre_ref)
        sim_ref[...] = jnp.zeros_like(sim_ref)

    u = u_ref[...]
    ub = u.astype(BF16)
    ys = []
    for c in range(SSM_CHUNKS):
        cs = slice(c * SSM_CHUNK_STATES, (c + 1) * SSM_CHUNK_STATES)
        uc = ub[:, c * LANES:(c + 1) * LANES]
        hre_ref[:, cs] = jnp.dot(uc, bre_ref[c], preferred_element_type=F32)
        him_ref[:, cs] = jnp.dot(uc, bim_ref[c], preferred_element_type=F32)
        h_re, h_im = sre_ref[:, cs], sim_ref[:, cs]
        l_re, l_im = lre_ref[:, cs], lim_ref[:, cs]
        for t in range(steps):
            rows = slice(t * batch, (t + 1) * batch)
            n_re = l_re * h_re - l_im * h_im + hre_ref[rows, cs]
            n_im = l_re * h_im + l_im * h_re + him_ref[rows, cs]
            hre_ref[rows, cs] = n_re
            him_ref[rows, cs] = n_im
            h_re, h_im = n_re, n_im
        sre_ref[:, cs] = h_re
        sim_ref[:, cs] = h_im
        ys.append(jnp.dot(hre_ref[:, cs].astype(BF16), cre_ref[c], preferred_element_type=F32)
                  - jnp.dot(him_ref[:, cs].astype(BF16), cim_ref[c], preferred_element_type=F32))
    y = jnp.concatenate(ys, axis=-1) + d_ref[...] * u
    y = jax.nn.gelu(y)
    gate = jnp.dot(y.astype(BF16), wg_ref[...], preferred_element_type=F32) + bg_ref[...]
    o_ref[...] = (y * jax.nn.sigmoid(gate)).astype(BF16)


def _ssm_mixer(u_tb, mats, d_skip, w_glu, b_glu, batch, steps):
    rows_total, width = u_tb.shape
    rows = steps * batch
    b_in_re, b_in_im, lam_re, lam_im, c_out_re, c_out_im = mats
    return pl.pallas_call(
        functools.partial(_ssm_kernel, batch=batch, steps=steps),
        grid=(rows_total // rows,),
        in_specs=[
            pl.BlockSpec((rows, width), lambda i: (i, 0)),
            _const_spec(b_in_re.shape), _const_spec(b_in_im.shape),
            _const_spec(lam_re.shape), _const_spec(lam_im.shape),
            _const_spec(c_out_re.shape), _const_spec(c_out_im.shape),
            _const_spec((1, width)), _const_spec((width, width)), _const_spec((1, width)),
        ],
        out_specs=pl.BlockSpec((rows, width), lambda i: (i, 0)),
        out_shape=jax.ShapeDtypeStruct((rows_total, width), BF16),
        scratch_shapes=[
            pltpu.VMEM((rows, SSM_STATES), F32), pltpu.VMEM((rows, SSM_STATES), F32),
            pltpu.VMEM((batch, SSM_STATES), F32), pltpu.VMEM((batch, SSM_STATES), F32),
        ],
        compiler_params=_params("arbitrary"),
        name="ssm_mixer",
    )(u_tb, b_in_re, b_in_im, lam_re, lam_im, c_out_re, c_out_im, d_skip, w_glu, b_glu)


def _odd_in_kernel(x_ref, g_ref, w_ref, q_ref, k_ref, v_ref):
    hn = _rmsnorm(x_ref[0], g_ref[...]).astype(BF16)
    proj = jnp.dot(hn, w_ref[...], preferred_element_type=F32)
    q_ref[0] = (proj[:, :SB_WIDTH] * (HEAD_DIM ** -0.5 * LOG2E)).astype(BF16)
    k_ref[0] = proj[:, SB_WIDTH:2 * SB_WIDTH].astype(BF16)
    v_ref[0] = proj[:, 2 * SB_WIDTH:].astype(BF16)


def _odd_in_proj(x, g, w, tm):
    b, l, d = x.shape
    row = pl.BlockSpec((1, tm, SB_WIDTH), lambda i, j: (i, j, 0))
    out = jax.ShapeDtypeStruct((b, l, SB_WIDTH), BF16)
    return pl.pallas_call(
        _odd_in_kernel,
        grid=(b, l // tm),
        in_specs=[pl.BlockSpec((1, tm, d), lambda i, j: (i, j, 0)),
                  _const_spec((1, d)), _const_spec((d, 3 * SB_WIDTH))],
        out_specs=[row, row, row],
        out_shape=[out, out, out],
        compiler_params=_params("parallel", "parallel"),
        name="odd_in_proj",
    )(x, g, w)


def _sb_streams(streams, tri):
    def scores(s):
        s["z"] = lax.dot_general(s["q"], s["k"], (((1,), (1,)), ((), ())),
                                 preferred_element_type=F32)

    def softplus(s):
        z = s["z"]
        sp = jnp.maximum(z, 0.0) + jnp.log(1.0 + jnp.exp2(-jnp.abs(z))) * LOG2E
        if s["valid"] is not None:
            sp = jnp.where(s["valid"], sp, 0.0)
        s["sp"] = sp.astype(BF16)

    def suffix_sums(s):
        s["w"] = jnp.dot(s.pop("sp"), tri, preferred_element_type=F32)

    def weights(s):
        p = jnp.exp2(s.pop("z") - s["w"])
        if s["valid"] is not None:
            p = jnp.where(s["valid"], p, 0.0)
        s["p"] = p.astype(BF16)

    def values(s):
        s["pv"] = jnp.dot(s.pop("p"), s["v"], preferred_element_type=F32)

    stages = (scores, softplus, suffix_sums, weights, values)
    state = [dict(q=q, k=kh, v=vb, valid=valid) for q, kh, vb, valid in streams]
    for stage in stages:
        for s in state:
            stage(s)
    return [(s["pv"], jnp.broadcast_to(s["w"][:, 0:1], (s["pv"].shape[0], LANES))) for s in state]


def _sb_kernel(q_ref, k_ref, v_ref, tri_ref, o_ref, c_ref, acc_ref, *, th):
    qi = pl.program_id(2)
    lo_k = lax.broadcasted_iota(jnp.int32, (th, LANES), 1) < HEAD_DIM
    lo_q = lax.broadcasted_iota(jnp.int32, (th, LANES), 1) < HEAD_DIM
    tri = tri_ref[...]
    strict = (lax.broadcasted_iota(jnp.int32, (th, th), 1)
              < lax.broadcasted_iota(jnp.int32, (th, th), 0))

    def load_kv(j):
        rows = pl.ds(pl.multiple_of(j * th, th), th)
        kb = k_ref[0, rows, :]
        return (jnp.where(lo_k, kb, 0), jnp.where(lo_k, 0, kb)), v_ref[0, rows, :]

    j_diag = (2 * qi, 2 * qi + 1)
    kv_left = load_kv(jnp.maximum(2 * qi - 1, 0))
    kv_diag = (load_kv(j_diag[0]), load_kv(j_diag[1]))
    has_left = (qi > 0).astype(F32)

    streams = []
    for half in range(2):
        q = q_ref[0, half * th:(half + 1) * th, :]
        (kd, vd) = kv_diag[half]
        (kl, vl) = kv_diag[0] if half == 1 else kv_left
        for h in range(2):
            streams += [(q, kd[h], vd, strict), (q, kl[h], vl, None)]
    outs = _sb_streams(streams, tri)
    for half in range(2):
        for h in range(2):
            (pv_d, c_d), (pv_l, c_l) = outs[4 * half + 2 * h], outs[4 * half + 2 * h + 1]
            scale = jnp.exp2(-c_d)
            if half == 0:
                scale, c_l = scale * has_left, c_l * has_left
            acc_ref[half, h] = pv_d + pv_l * scale
            c_ref[half, h] = c_d + c_l

    def visit_rest(half):
        q = q_ref[0, half * th:(half + 1) * th, :]
        n_rest = jnp.maximum(j_diag[half] - 1, 0)

        def cond(carry):
            i, c_min = carry
            return jnp.logical_and(i < n_rest, c_min < SB_DEAD_LOG2)

        def body(carry):
            i, _ = carry
            ks, vb = load_kv(n_rest - 1 - i)
            outs = _sb_streams([(q, ks[0], vb, None), (q, ks[1], vb, None)], tri)
            for h, (pv, c_blk) in enumerate(outs):
                c = c_ref[half, h]
                acc_ref[half, h] += pv * jnp.exp2(-c)
                c_ref[half, h] = c + c_blk
            return i + 1, jnp.min(c_ref[half])

        lax.while_loop(cond, body, (jnp.int32(0), jnp.min(c_ref[half])))

    @pl.when(jnp.min(c_ref[...]) < SB_DEAD_LOG2)
    def _():
        for half in range(2):
            visit_rest(half)

    for half in range(2):
        o_ref[0, half * th:(half + 1) * th, :] = jnp.where(
            lo_q, acc_ref[half, 0], acc_ref[half, 1]).astype(BF16)


def _sb_attention(q, k, v, th):
    b, l, width = q.shape
    tri = (jnp.arange(th)[:, None] >= jnp.arange(th)[None, :]).astype(BF16)
    tq = 2 * th
    return pl.pallas_call(
        functools.partial(_sb_kernel, th=th),
        grid=(b, width // LANES, l // tq),
        in_specs=[
            pl.BlockSpec((1, tq, LANES), lambda i, h, j: (i, j, h)),
            pl.BlockSpec((1, l, LANES), lambda i, h, j: (i, 0, h)),
            pl.BlockSpec((1, l, LANES), lambda i, h, j: (i, 0, h)),
            _const_spec((th, th)),
        ],
        out_specs=pl.BlockSpec((1, tq, LANES), lambda i, h, j: (i, j, h)),
        out_shape=jax.ShapeDtypeStruct((b, l, width), BF16),
        scratch_shapes=[pltpu.VMEM((2, 2, th, LANES), F32), pltpu.VMEM((2, 2, th, LANES), F32)],
        compiler_params=_params("parallel", "parallel", "arbitrary"),
        name="sb_attention",
    )(q, k, v, tri)


def _post_kernel(*refs, n_mix):
    x_ref = refs[0]
    mix_refs = refs[1:1 + n_mix]
    wo_refs = refs[1 + n_mix:1 + 2 * n_mix]
    g_ref, wg_ref, wu_ref, wd_ref, o_ref = refs[1 + 2 * n_mix:]
    x = x_ref[0]
    for m_ref, w_ref in zip(mix_refs, wo_refs):
        m = m_ref[0] if len(m_ref.shape) == 3 else m_ref[...]
        x = x + jnp.dot(m, w_ref[...], preferred_element_type=F32)
    hn = _rmsnorm(x, g_ref[...]).astype(BF16)
    gate = jnp.dot(hn, wg_ref[...], preferred_element_type=F32)
    up = jnp.dot(hn, wu_ref[...], preferred_element_type=F32)
    act = (gate * jax.nn.sigmoid(gate) * up).astype(BF16)
    o_ref[0] = x + jnp.dot(act, wd_ref[...], preferred_element_type=F32)


def _post(x, mixes, mix_specs, w_outs, g, w_gate, w_up, w_down, tm):
    b, l, d = x.shape
    xspec = pl.BlockSpec((1, tm, d), lambda i, j: (i, j, 0))
    return pl.pallas_call(
        functools.partial(_post_kernel, n_mix=len(mixes)),
        grid=(b, l // tm),
        in_specs=[xspec, *mix_specs, *[_const_spec(w.shape) for w in w_outs],
                  _const_spec((1, d)), _const_spec(w_gate.shape), _const_spec(w_up.shape),
                  _const_spec(w_down.shape)],
        out_specs=xspec,
        out_shape=jax.ShapeDtypeStruct(x.shape, F32),
        compiler_params=_params("parallel", "parallel"),
        name="out_proj_ffn",
    )(x, *mixes, *w_outs, g, w_gate, w_up, w_down)


def _tiles(l):
    tm = min(512, l)
    th = min(256, l // 2)
    steps = min(64, l)
    nsub = min(2, l // SWA_BLOCK)
    return tm, th, steps, nsub


def kernel(x, even_norm, even_w_in, q_norm, k_norm, sinks, ssm_a_re, ssm_a_im, ssm_b_re, ssm_b_im,
           ssm_c_re, ssm_c_im, ssm_d, ssm_log_dt, ssm_w_glu, ssm_b_glu, even_w_out, odd_norm,
           odd_w_in, odd_w_out, ffn_norm, ffn_w_gate, ffn_w_up, ffn_w_down):
    b, l, d = x.shape
    assert d == D_MODEL and l % SWA_BLOCK == 0
    tm, th, steps, nsub = _tiles(l)
    depth = ffn_norm.shape[0]
    bf = lambda t: t.astype(BF16)
    row = lambda t: t.reshape(1, -1)
    for layer in range(depth):
        i = layer // 2
        if layer % 2 == 0:
            q, k, v, u_t = _even_in_proj(
                x, row(even_norm[i]), bf(even_w_in[i]),
                row(jnp.tile(q_norm[i], 2)), row(jnp.tile(k_norm[i], 2)), tm)
            o_attn = _swa_attention(sinks[i], q, k, v, nsub)
            mats = _ssm_prepare(ssm_a_re[i], ssm_a_im[i], ssm_b_re[i], ssm_b_im[i],
                                ssm_c_re[i], ssm_c_im[i], ssm_log_dt[i], b)
            o_ssm = _ssm_mixer(u_t.reshape(l * b, SSM_WIDTH), mats, row(ssm_d[i]),
                               bf(ssm_w_glu[i]), row(ssm_b_glu[i]), b, steps)
            mixes = [o_attn, o_ssm.reshape(l, b * SSM_WIDTH)]
            mix_specs = [pl.BlockSpec((1, tm, SWA_WIDTH), lambda bi, j: (bi, j, 0)),
                         pl.BlockSpec((tm, SSM_WIDTH), lambda bi, j: (j, bi))]
            w_outs = [bf(even_w_out[i][:SWA_WIDTH]), bf(even_w_out[i][SWA_WIDTH:])]
        else:
            q, k, v = _odd_in_proj(x, row(odd_norm[i]), bf(odd_w_in[i]), tm)
            mixes = [_sb_attention(q, k, v, th)]
            mix_specs = [pl.BlockSpec((1, tm, SB_WIDTH), lambda bi, j: (bi, j, 0))]
            w_outs = [bf(odd_w_out[i])]
        x = _post(x, mixes, mix_specs, w_outs, row(ffn_norm[layer]), bf(ffn_w_gate[layer]),
                  bf(ffn_w_up[layer]), bf(ffn_w_down[layer]), tm)
    return x
```

```python
import functools
import math

import jax
import jax.numpy as jnp
from jax import lax
from jax.experimental import pallas as pl
from jax.experimental.pallas import tpu as pltpu

F32 = jnp.float32
BF16 = jnp.bfloat16

D_MODEL = 1024
HEAD_DIM = 64
EPS = 1e-6
LANES = 128

SWA_Q_HEADS = 8
SWA_KV_HEADS = 2
SWA_BLOCK = 128
SWA_WIDTH = SWA_Q_HEADS * HEAD_DIM
KV_WIDTH = SWA_KV_HEADS * HEAD_DIM

SSM_WIDTH = D_MODEL // 2
SSM_GROUP_CH = 16
SSM_GROUPS = SSM_WIDTH // SSM_GROUP_CH
SSM_STATE = 64
SSM_STATES = SSM_GROUPS * SSM_STATE
SSM_CHUNKS = SSM_WIDTH // LANES
SSM_CHUNK_STATES = SSM_STATES // SSM_CHUNKS
EVEN_IN = SWA_WIDTH + 2 * KV_WIDTH + SSM_WIDTH

SB_HEADS = D_MODEL // HEAD_DIM
SB_WIDTH = SB_HEADS * HEAD_DIM
LOG2E = math.log2(math.e)
SB_DEAD_LOG2 = 160.0
SB_PAIRS_PER_STEP = 2
SB_STREAM_GROUP = 4

VMEM_LIMIT_BYTES = 56 * 1024 * 1024


def _params(*sem):
    return pltpu.CompilerParams(dimension_semantics=sem, vmem_limit_bytes=VMEM_LIMIT_BYTES)


def _const_spec(shape):
    nd = len(shape)
    return pl.BlockSpec(shape, lambda *_: (0,) * nd, pipeline_mode=pl.Buffered(1))


def _rmsnorm(x, g):
    ms = jnp.mean(x * x, axis=-1, keepdims=True)
    return x * lax.rsqrt(ms + EPS) * g


def _pair_rmsnorm(x, g2):
    lo = lax.broadcasted_iota(jnp.int32, x.shape, 1) < HEAD_DIM
    sq = x * x
    s_lo = jnp.sum(jnp.where(lo, sq, 0.0), axis=-1, keepdims=True)
    s_hi = jnp.sum(jnp.where(lo, 0.0, sq), axis=-1, keepdims=True)
    ms = jnp.where(lo, s_lo, s_hi) * (1.0 / HEAD_DIM)
    return x * lax.rsqrt(ms + EPS) * g2


def _even_in_kernel(x_ref, g_ref, w_ref, qg_ref, kg_ref, q_ref, k_ref, v_ref, u_ref):
    hn = _rmsnorm(x_ref[0], g_ref[...]).astype(BF16)
    proj = jnp.dot(hn, w_ref[...], preferred_element_type=F32)
    scale = HEAD_DIM ** -0.5 * LOG2E
    for p in range(SWA_WIDTH // LANES):
        qp = _pair_rmsnorm(proj[:, p * LANES:(p + 1) * LANES], qg_ref[...])
        q_ref[0, :, p * LANES:(p + 1) * LANES] = (qp * scale).astype(BF16)
    k = _pair_rmsnorm(proj[:, SWA_WIDTH:SWA_WIDTH + KV_WIDTH], kg_ref[...])
    k_ref[0] = k.astype(BF16)
    v_ref[0] = proj[:, SWA_WIDTH + KV_WIDTH:SWA_WIDTH + 2 * KV_WIDTH].astype(BF16)
    u_ref[0] = proj[:, SWA_WIDTH + 2 * KV_WIDTH:]


def _even_in_proj(x, g, w, qg, kg, tm):
    b, l, d = x.shape
    return pl.pallas_call(
        _even_in_kernel,
        grid=(b, l // tm),
        in_specs=[
            pl.BlockSpec((1, tm, d), lambda i, j: (i, j, 0)),
            _const_spec((1, d)),
            _const_spec((d, EVEN_IN)),
            _const_spec((1, LANES)),
            _const_spec((1, LANES)),
        ],
        out_specs=[
            pl.BlockSpec((1, tm, SWA_WIDTH), lambda i, j: (i, j, 0)),
            pl.BlockSpec((1, tm, KV_WIDTH), lambda i, j: (i, j, 0)),
            pl.BlockSpec((1, tm, KV_WIDTH), lambda i, j: (i, j, 0)),
            pl.BlockSpec((1, tm, SSM_WIDTH), lambda i, j: (i, j, 0)),
        ],
        out_shape=[
            jax.ShapeDtypeStruct((b, l, SWA_WIDTH), BF16),
            jax.ShapeDtypeStruct((b, l, KV_WIDTH), BF16),
            jax.ShapeDtypeStruct((b, l, KV_WIDTH), BF16),
            jax.ShapeDtypeStruct((b, l, SSM_WIDTH), F32),
        ],
        compiler_params=_params("parallel", "parallel"),
        name="even_in_proj",
    )(x, g, w, qg, kg)


def _swa_kernel(sink_ref, q_ref, kc_ref, kp_ref, vc_ref, vp_ref, o_ref, *, nsub):
    n = pl.program_id(1)
    blk = SWA_BLOCK
    kall = jnp.concatenate([kp_ref[0], kc_ref[0]], axis=0).astype(F32)
    vall = jnp.concatenate([vp_ref[0], vc_ref[0]], axis=0).astype(F32)
    lo = lax.broadcasted_iota(jnp.int32, kall.shape, 1) < HEAD_DIM

    def halves(t):
        g0_lo = jnp.where(lo, t, 0.0)
        g1_hi = jnp.where(lo, 0.0, t)
        g0_hi = pltpu.roll(g0_lo, HEAD_DIM, 1)
        g1_lo = pltpu.roll(g1_hi, HEAD_DIM, 1)
        return [[g0_lo.astype(BF16), g0_hi.astype(BF16)], [g1_lo.astype(BF16), g1_hi.astype(BF16)]]

    ks, vs = halves(kall), halves(vall)
    qq = lax.broadcasted_iota(jnp.int32, (blk, 2 * blk), 0)
    kk = lax.broadcasted_iota(jnp.int32, (blk, 2 * blk), 1)
    diff = qq + blk - kk
    band = (diff >= 0) & (diff < blk)
    first_key = jnp.where(n > 0, 0, blk)
    band_first = band & (kk >= first_key)

    streams = []
    for s in range(nsub):
        keys = slice(s * blk, (s + 2) * blk)
        for p in range(SWA_WIDTH // LANES):
            g = (2 * p) // (SWA_Q_HEADS // SWA_KV_HEADS)
            for e in range(2):
                streams.append(dict(
                    q=q_ref[0, s * blk:(s + 1) * blk, p * LANES:(p + 1) * LANES],
                    k=ks[g][e][keys], v=vs[g][e][keys],
                    sink=sink_ref[2 * p + e] * LOG2E, mask=band_first if s == 0 else band))
    group = len(streams) // nsub
    for g0 in range(0, len(streams), group):
        for st in streams[g0:g0 + group]:
            st["z"] = lax.dot_general(st["q"], st["k"], (((1,), (1,)), ((), ())),
                                      preferred_element_type=F32)
        for st in streams[g0:g0 + group]:
            z = jnp.where(st["mask"], st.pop("z"), -jnp.inf)
            m = jnp.maximum(jnp.max(z, axis=-1, keepdims=True), st["sink"])
            pe = jnp.exp2(z - m)
            st["den"] = jnp.sum(pe, axis=-1, keepdims=True) + jnp.exp2(st["sink"] - m)
            st["pe"] = pe.astype(BF16)
        for st in streams[g0:g0 + group]:
            st["o"] = jnp.dot(st.pop("pe"), st["v"], preferred_element_type=F32) / st["den"]
    for s in range(nsub):
        for p in range(SWA_WIDTH // LANES):
            i = (s * (SWA_WIDTH // LANES) + p) * 2
            o_ref[0, s * blk:(s + 1) * blk, p * LANES:(p + 1) * LANES] = (
                streams[i]["o"] + streams[i + 1]["o"]).astype(BF16)


def _swa_attention(sinks, q, k, v, nsub):
    b, l, _ = q.shape
    blk = SWA_BLOCK * nsub
    cur = lambda i, j: (i, j, 0)
    prev = lambda i, j: (i, jnp.maximum(j * nsub - 1, 0), 0)
    return pl.pallas_call(
        functools.partial(_swa_kernel, nsub=nsub),
        grid=(b, l // blk),
        in_specs=[
            pl.BlockSpec(memory_space=pltpu.SMEM),
            pl.BlockSpec((1, blk, SWA_WIDTH), cur),
            pl.BlockSpec((1, blk, KV_WIDTH), cur),
            pl.BlockSpec((1, SWA_BLOCK, KV_WIDTH), prev),
            pl.BlockSpec((1, blk, KV_WIDTH), cur),
            pl.BlockSpec((1, SWA_BLOCK, KV_WIDTH), prev),
        ],
        out_specs=pl.BlockSpec((1, blk, SWA_WIDTH), cur),
        out_shape=jax.ShapeDtypeStruct((b, l, SWA_WIDTH), BF16),
        compiler_params=_params("parallel", "parallel"),
        name="swa_attention",
    )(sinks, q, k, k, v, v)


def _ssm_disc_kernel(are_ref, aim_ref, ldt_ref, lre_ref, lim_ref, wre_ref, wim_ref):
    a_re, a_im = are_ref[...], aim_ref[...]
    dt = jnp.exp(ldt_ref[...])
    mag = jnp.exp(a_re * dt)
    lam_re = mag * jnp.cos(a_im * dt)
    lam_im = mag * jnp.sin(a_im * dt)
    den = a_re * a_re + a_im * a_im
    lre_ref[...] = lam_re
    lim_ref[...] = lam_im
    wre_ref[...] = ((lam_re - 1.0) * a_re + lam_im * a_im) / den
    wim_ref[...] = (lam_im * a_re - (lam_re - 1.0) * a_im) / den


def _ssm_bbar_kernel(wre_ref, wim_ref, bre_ref, bim_ref, ore_ref, oim_ref):
    w_re, w_im, b_re, b_im = wre_ref[...], wim_ref[...], bre_ref[...], bim_ref[...]
    ore_ref[...] = w_re * b_re - w_im * b_im
    oim_ref[...] = w_re * b_im + w_im * b_re


def _block_diag(t):
    ch, gl, r, c = t.shape
    eye = jnp.eye(gl, dtype=t.dtype)
    return (t[:, :, :, None, :] * eye[None, :, None, :, None]).reshape(ch, gl * r, gl * c)


def _ssm_prepare(a_re, a_im, b_re, b_im, c_re, c_im, log_dt, batch):
    g, n, p = SSM_GROUPS, SSM_STATE, SSM_GROUP_CH
    gn = jax.ShapeDtypeStruct((g, n), F32)
    ldt = jnp.broadcast_to(log_dt[:, None], (g, n))
    lam_re, lam_im, w_re, w_im = pl.pallas_call(
        _ssm_disc_kernel, out_shape=[gn, gn, gn, gn], name="ssm_discretise")(a_re, a_im, ldt)
    gnp = jax.ShapeDtypeStruct((g, n * p), F32)
    bb_re, bb_im = pl.pallas_call(_ssm_bbar_kernel, out_shape=[gnp, gnp], name="ssm_bbar")(
        jnp.repeat(w_re, p, axis=1), jnp.repeat(w_im, p, axis=1),
        b_re.reshape(g, n * p), b_im.reshape(g, n * p))
    gl = g // SSM_CHUNKS
    to_in = lambda t: _block_diag(
        t.reshape(SSM_CHUNKS, gl, n, p).transpose(0, 1, 3, 2)).astype(BF16)
    to_out = lambda t: _block_diag(
        t.reshape(SSM_CHUNKS, gl, p, n).transpose(0, 1, 3, 2)).astype(BF16)
    bcast = lambda t: jnp.broadcast_to(t.reshape(1, SSM_STATES), (batch, SSM_STATES))
    return (to_in(bb_re), to_in(bb_im), bcast(lam_re), bcast(lam_im), to_out(c_re), to_out(c_im))


def _ssm_kernel(u_ref, bre_ref, bim_ref, lre_ref, lim_ref, cre_ref, cim_ref, d_ref, wg_ref, bg_ref,
                o_ref, ubuf_ref, h_ref, sre_ref, sim_ref, *, batch, steps, pitch):
    @pl.when(pl.program_id(0) == 0)
    def _():
        sre_ref[...] = jnp.zeros_like(sre_ref)
        sim_ref[...] = jnp.zeros_like(sim_ref)
        ubuf_ref[...] = jnp.zeros_like(ubuf_ref)

    for b in range(batch):
        ubuf_ref[b * pitch:b * pitch + steps, :] = u_ref[b]
    u = ubuf_ref[...]
    ub = u.astype(BF16)
    tiles = SSM_CHUNK_STATES // LANES
    ys = []
    for c in range(SSM_CHUNKS):
        uc = ub[:, c * LANES:(c + 1) * LANES]
        bu = (jnp.dot(uc, bre_ref[c], preferred_element_type=F32),
              jnp.dot(uc, bim_ref[c], preferred_element_type=F32))
        for part in range(2):
            for i in range(tiles):
                h_ref[part, c * tiles + i] = bu[part][:, i * LANES:(i + 1) * LANES]
        lanes = [slice(c * SSM_CHUNK_STATES + i * LANES, c * SSM_CHUNK_STATES + (i + 1) * LANES)
                 for i in range(tiles)]
        h_re = [sre_ref[:, ln] for ln in lanes]
        h_im = [sim_ref[:, ln] for ln in lanes]
        l_re = [lre_ref[:, ln] for ln in lanes]
        l_im = [lim_ref[:, ln] for ln in lanes]
        for t in range(steps):
            rows = pl.ds(t, batch, stride=pitch)
            for i in range(tiles):
                n_re = l_re[i] * h_re[i] - l_im[i] * h_im[i] + h_ref[0, c * tiles + i, rows, :]
                n_im = l_re[i] * h_im[i] + l_im[i] * h_re[i] + h_ref[1, c * tiles + i, rows, :]
                h_ref[0, c * tiles + i, rows, :] = n_re
                h_ref[1, c * tiles + i, rows, :] = n_im
                h_re[i], h_im[i] = n_re, n_im
        for i, ln in enumerate(lanes):
            sre_ref[:, ln] = h_re[i]
            sim_ref[:, ln] = h_im[i]
        hs = [jnp.concatenate([h_ref[part, c * tiles + i] for i in range(tiles)], axis=1).astype(BF16)
              for part in range(2)]
        ys.append(jnp.dot(hs[0], cre_ref[c], preferred_element_type=F32)
                  - jnp.dot(hs[1], cim_ref[c], preferred_element_type=F32))
    y = jnp.concatenate(ys, axis=-1) + d_ref[...] * u
    y = jax.nn.gelu(y)
    gate = jnp.dot(y.astype(BF16), wg_ref[...], preferred_element_type=F32) + bg_ref[...]
    out = y * jax.nn.sigmoid(gate)
    for b in range(batch):
        o_ref[b] = out[b * pitch:b * pitch + steps].astype(BF16)


def _ssm_mixer(u, mats, d_skip, w_glu, b_glu, steps):
    batch, l, width = u.shape
    assert steps % 8 == 0
    pitch = steps + 4
    b_in_re, b_in_im, lam_re, lam_im, c_out_re, c_out_im = mats
    block = pl.BlockSpec((batch, steps, width), lambda i: (0, i, 0))
    return pl.pallas_call(
        functools.partial(_ssm_kernel, batch=batch, steps=steps, pitch=pitch),
        grid=(l // steps,),
        in_specs=[
            block,
            _const_spec(b_in_re.shape), _const_spec(b_in_im.shape),
            _const_spec(lam_re.shape), _const_spec(lam_im.shape),
            _const_spec(c_out_re.shape), _const_spec(c_out_im.shape),
            _const_spec((1, width)), _const_spec((width, width)), _const_spec((1, width)),
        ],
        out_specs=block,
        out_shape=jax.ShapeDtypeStruct((batch, l, width), BF16),
        scratch_shapes=[
            pltpu.VMEM((batch * pitch, width), F32),
            pltpu.VMEM((2, SSM_STATES // LANES, batch * pitch, LANES), F32),
            pltpu.VMEM((batch, SSM_STATES), F32), pltpu.VMEM((batch, SSM_STATES), F32),
        ],
        compiler_params=_params("arbitrary"),
        name="ssm_mixer",
    )(u, b_in_re, b_in_im, lam_re, lam_im, c_out_re, c_out_im, d_skip, w_glu, b_glu)


def _odd_in_kernel(x_ref, g_ref, w_ref, q_ref, k_ref, v_ref):
    hn = _rmsnorm(x_ref[0], g_ref[...]).astype(BF16)
    proj = jnp.dot(hn, w_ref[...], preferred_element_type=F32)
    q_ref[0] = (proj[:, :SB_WIDTH] * (HEAD_DIM ** -0.5 * LOG2E)).astype(BF16)
    k_ref[0] = proj[:, SB_WIDTH:2 * SB_WIDTH].astype(BF16)
    v_ref[0] = proj[:, 2 * SB_WIDTH:].astype(BF16)


def _odd_in_proj(x, g, w, tm):
    b, l, d = x.shape
    row = pl.BlockSpec((1, tm, SB_WIDTH), lambda i, j: (i, j, 0))
    out = jax.ShapeDtypeStruct((b, l, SB_WIDTH), BF16)
    return pl.pallas_call(
        _odd_in_kernel,
        grid=(b, l // tm),
        in_specs=[pl.BlockSpec((1, tm, d), lambda i, j: (i, j, 0)),
                  _const_spec((1, d)), _const_spec((d, 3 * SB_WIDTH))],
        out_specs=[row, row, row],
        out_shape=[out, out, out],
        compiler_params=_params("parallel", "parallel"),
        name="odd_in_proj",
    )(x, g, w)


def _sb_streams(streams, tri):
    def scores(s):
        s["z"] = lax.dot_general(s["q"], s["k"], (((1,), (1,)), ((), ())),
                                 preferred_element_type=F32)

    def softplus(s):
        z = s["z"]
        sp = jnp.maximum(z, 0.0) + jnp.log(1.0 + jnp.exp2(-jnp.abs(z))) * LOG2E
        if s["valid"] is not None:
            sp = jnp.where(s["valid"], sp, 0.0)
        s["sp"] = sp.astype(BF16)

    def suffix_sums(s):
        s["w"] = jnp.dot(s.pop("sp"), tri, preferred_element_type=F32)

    def weights(s):
        p = jnp.exp2(s.pop("z") - s["w"])
        if s["valid"] is not None:
            p = jnp.where(s["valid"], p, 0.0)
        s["p"] = p.astype(BF16)

    def values(s):
        s["pv"] = jnp.dot(s.pop("p"), s["v"], preferred_element_type=F32)

    stages = (scores, softplus, suffix_sums, weights, values)
    state = [dict(q=q, k=kh, v=vb, valid=valid) for q, kh, vb, valid in streams]
    groups = [state[i:i + SB_STREAM_GROUP] for i in range(0, len(state), SB_STREAM_GROUP)]
    for tick in range(len(groups) + len(stages) - 1):
        for g, members in enumerate(groups):
            if 0 <= tick - g < len(stages):
                for s in members:
                    stages[tick - g](s)
    return [(s["pv"], jnp.broadcast_to(s["w"][:, 0:1], (s["pv"].shape[0], LANES))) for s in state]


def _sb_kernel(q_ref, k_ref, v_ref, tri_ref, o_ref, c_ref, acc_ref, *, th, npair):
    qi = pl.program_id(2)
    lo_k = lax.broadcasted_iota(jnp.int32, (th, LANES), 1) < HEAD_DIM
    lo_q = lax.broadcasted_iota(jnp.int32, (th, LANES), 1) < HEAD_DIM
    tri = tri_ref[...]
    strict = (lax.broadcasted_iota(jnp.int32, (th, th), 1)
              < lax.broadcasted_iota(jnp.int32, (th, th), 0))

    def load_kv(pair, j):
        rows = pl.ds(pl.multiple_of(j * th, th), th)
        lanes = slice(pair * LANES, (pair + 1) * LANES)
        kb = k_ref[0, rows, lanes]
        return (jnp.where(lo_k, kb, 0), jnp.where(lo_k, 0, kb)), v_ref[0, rows, lanes]

    def load_q(pair, half):
        return q_ref[0, half * th:(half + 1) * th, pair * LANES:(pair + 1) * LANES]

    j_diag = (2 * qi, 2 * qi + 1)
    has_left = (qi > 0).astype(F32)

    streams = []
    for pair in range(npair):
        kv_left = load_kv(pair, jnp.maximum(2 * qi - 1, 0))
        kv_diag = (load_kv(pair, j_diag[0]), load_kv(pair, j_diag[1]))
        for half in range(2):
            q = load_q(pair, half)
            (kd, vd) = kv_diag[half]
            (kl, vl) = kv_diag[0] if half == 1 else kv_left
            for h in range(2):
                streams += [(q, kd[h], vd, strict), (q, kl[h], vl, None)]
    outs = iter(_sb_streams(streams, tri))
    for pair in range(npair):
        for half in range(2):
            for h in range(2):
                (pv_d, c_d), (pv_l, c_l) = next(outs), next(outs)
                scale = jnp.exp2(-c_d)
                if half == 0:
                    scale, c_l = scale * has_left, c_l * has_left
                acc_ref[pair, half, h] = pv_d + pv_l * scale
                c_ref[pair, half, h] = c_d + c_l

    def visit_rest(pair, half):
        q = load_q(pair, half)
        n_rest = jnp.maximum(j_diag[half] - 1, 0)

        def cond(carry):
            i, c_min = carry
            return jnp.logical_and(i < n_rest, c_min < SB_DEAD_LOG2)

        def body(carry):
            i, _ = carry
            ks, vb = load_kv(pair, n_rest - 1 - i)
            outs = _sb_streams([(q, ks[0], vb, None), (q, ks[1], vb, None)], tri)
            for h, (pv, c_blk) in enumerate(outs):
                c = c_ref[pair, half, h]
                acc_ref[pair, half, h] += pv * jnp.exp2(-c)
                c_ref[pair, half, h] = c + c_blk
            return i + 1, jnp.min(c_ref[pair, half])

        lax.while_loop(cond, body, (jnp.int32(0), jnp.min(c_ref[pair, half])))

    @pl.when(jnp.min(c_ref[...]) < SB_DEAD_LOG2)
    def _():
        for pair in range(npair):
            for half in range(2):
                visit_rest(pair, half)

    for pair in range(npair):
        for half in range(2):
            o_ref[0, half * th:(half + 1) * th, pair * LANES:(pair + 1) * LANES] = jnp.where(
                lo_q, acc_ref[pair, half, 0], acc_ref[pair, half, 1]).astype(BF16)


def _sb_attention(q, k, v, th, npair):
    b, l, width = q.shape
    tri = (jnp.arange(th)[:, None] >= jnp.arange(th)[None, :]).astype(BF16)
    tq = 2 * th
    lanes = npair * LANES
    state = pltpu.VMEM((npair, 2, 2, th, LANES), F32)
    return pl.pallas_call(
        functools.partial(_sb_kernel, th=th, npair=npair),
        grid=(b, width // lanes, l // tq),
        in_specs=[
            pl.BlockSpec((1, tq, lanes), lambda i, h, j: (i, j, h)),
            pl.BlockSpec((1, l, lanes), lambda i, h, j: (i, 0, h)),
            pl.BlockSpec((1, l, lanes), lambda i, h, j: (i, 0, h)),
            _const_spec((th, th)),
        ],
        out_specs=pl.BlockSpec((1, tq, lanes), lambda i, h, j: (i, j, h)),
        out_shape=jax.ShapeDtypeStruct((b, l, width), BF16),
        scratch_shapes=[state, state],
        compiler_params=_params("parallel", "parallel", "arbitrary"),
        name="sb_attention",
    )(q, k, v, tri)


def _post_kernel(*refs, n_mix):
    x_ref = refs[0]
    mix_refs = refs[1:1 + n_mix]
    wo_refs = refs[1 + n_mix:1 + 2 * n_mix]
    g_ref, wg_ref, wu_ref, wd_ref, o_ref = refs[1 + 2 * n_mix:]
    x = x_ref[0]
    for m_ref, w_ref in zip(mix_refs, wo_refs):
        x = x + jnp.dot(m_ref[0], w_ref[...], preferred_element_type=F32)
    hn = _rmsnorm(x, g_ref[...]).astype(BF16)
    gate = jnp.dot(hn, wg_ref[...], preferred_element_type=F32)
    up = jnp.dot(hn, wu_ref[...], preferred_element_type=F32)
    act = (gate * jax.nn.sigmoid(gate) * up).astype(BF16)
    o_ref[0] = x + jnp.dot(act, wd_ref[...], preferred_element_type=F32)


def _post(x, mixes, mix_specs, w_outs, g, w_gate, w_up, w_down, tm):
    b, l, d = x.shape
    xspec = pl.BlockSpec((1, tm, d), lambda i, j: (i, j, 0))
    return pl.pallas_call(
        functools.partial(_post_kernel, n_mix=len(mixes)),
        grid=(b, l // tm),
        in_specs=[xspec, *mix_specs, *[_const_spec(w.shape) for w in w_outs],
                  _const_spec((1, d)), _const_spec(w_gate.shape), _const_spec(w_up.shape),
                  _const_spec(w_down.shape)],
        out_specs=xspec,
        out_shape=jax.ShapeDtypeStruct(x.shape, F32),
        compiler_params=_params("parallel", "parallel"),
        name="out_proj_ffn",
    )(x, *mixes, *w_outs, g, w_gate, w_up, w_down)


def _tiles(l):
    tm = min(512, l)
    th = min(256, l // 2)
    steps = min(64, l)
    nsub = min(2, l // SWA_BLOCK)
    return tm, th, steps, nsub


def kernel(x, even_norm, even_w_in, q_norm, k_norm, sinks, ssm_a_re, ssm_a_im, ssm_b_re, ssm_b_im,
           ssm_c_re, ssm_c_im, ssm_d, ssm_log_dt, ssm_w_glu, ssm_b_glu, even_w_out, odd_norm,
           odd_w_in, odd_w_out, ffn_norm, ffn_w_gate, ffn_w_up, ffn_w_down):
    b, l, d = x.shape
    assert d == D_MODEL and l % SWA_BLOCK == 0
    tm, th, steps, nsub = _tiles(l)
    depth = ffn_norm.shape[0]
    bf = lambda t: t.astype(BF16)
    row = lambda t: t.reshape(1, -1)
    for layer in range(depth):
        i = layer // 2
        if layer % 2 == 0:
            q, k, v, u = _even_in_proj(
                x, row(even_norm[i]), bf(even_w_in[i]),
                row(jnp.tile(q_norm[i], 2)), row(jnp.tile(k_norm[i], 2)), tm)
            o_attn = _swa_attention(sinks[i], q, k, v, nsub)
            mats = _ssm_prepare(ssm_a_re[i], ssm_a_im[i], ssm_b_re[i], ssm_b_im[i],
                                ssm_c_re[i], ssm_c_im[i], ssm_log_dt[i], b)
            o_ssm = _ssm_mixer(u, mats, row(ssm_d[i]), bf(ssm_w_glu[i]), row(ssm_b_glu[i]), steps)
            mixes = [o_attn, o_ssm]
            mix_specs = [pl.BlockSpec((1, tm, SWA_WIDTH), lambda bi, j: (bi, j, 0)),
                         pl.BlockSpec((1, tm, SSM_WIDTH), lambda bi, j: (bi, j, 0))]
            w_outs = [bf(even_w_out[i][:SWA_WIDTH]), bf(even_w_out[i][SWA_WIDTH:])]
        else:
            q, k, v = _odd_in_proj(x, row(odd_norm[i]), bf(odd_w_in[i]), tm)
            mixes = [_sb_attention(q, k, v, th, SB_PAIRS_PER_STEP)]
            mix_specs = [pl.BlockSpec((1, tm, SB_WIDTH), lambda bi, j: (bi, j, 0))]
            w_outs = [bf(odd_w_out[i])]
        x = _post(x, mixes, mix_specs, w_outs, row(ffn_norm[layer]), bf(ffn_w_gate[layer]),
                  bf(ffn_w_up[layer]), bf(ffn_w_down[layer]), tm)
    return x
```

```python
import functools
import math

import jax
import jax.numpy as jnp
from jax import lax
from jax.experimental import pallas as pl
from jax.experimental.pallas import tpu as pltpu

F32 = jnp.float32
BF16 = jnp.bfloat16

D_MODEL = 1024
HEAD_DIM = 64
EPS = 1e-6
LANES = 128

SWA_Q_HEADS = 8
SWA_KV_HEADS = 2
SWA_BLOCK = 128
SWA_WIDTH = SWA_Q_HEADS * HEAD_DIM
KV_WIDTH = SWA_KV_HEADS * HEAD_DIM

SSM_WIDTH = D_MODEL // 2
SSM_GROUP_CH = 16
SSM_GROUPS = SSM_WIDTH // SSM_GROUP_CH
SSM_STATE = 64
SSM_STATES = SSM_GROUPS * SSM_STATE
SSM_CHUNKS = SSM_WIDTH // LANES
SSM_CHUNK_STATES = SSM_STATES // SSM_CHUNKS
EVEN_IN = SWA_WIDTH + 2 * KV_WIDTH + SSM_WIDTH

SB_HEADS = D_MODEL // HEAD_DIM
SB_WIDTH = SB_HEADS * HEAD_DIM
LOG2E = math.log2(math.e)
SB_DEAD_LOG2 = 160.0
SB_EXP2_CLAMP = 126.0
SB_PAIRS_PER_STEP = 2
SB_STREAM_GROUP = 2

VMEM_LIMIT_BYTES = 56 * 1024 * 1024


def _params(*sem):
    return pltpu.CompilerParams(dimension_semantics=sem, vmem_limit_bytes=VMEM_LIMIT_BYTES)


def _const_spec(shape):
    nd = len(shape)
    return pl.BlockSpec(shape, lambda *_: (0,) * nd, pipeline_mode=pl.Buffered(1))


def _rmsnorm(x, g):
    ms = jnp.mean(x * x, axis=-1, keepdims=True)
    return x * lax.rsqrt(ms + EPS) * g


def _pair_rmsnorm(x, g2):
    lo = lax.broadcasted_iota(jnp.int32, x.shape, 1) < HEAD_DIM
    sq = x * x
    s_lo = jnp.sum(jnp.where(lo, sq, 0.0), axis=-1, keepdims=True)
    s_hi = jnp.sum(jnp.where(lo, 0.0, sq), axis=-1, keepdims=True)
    ms = jnp.where(lo, s_lo, s_hi) * (1.0 / HEAD_DIM)
    return x * lax.rsqrt(ms + EPS) * g2


def _even_in_kernel(x_ref, g_ref, w_ref, qg_ref, kg_ref, q_ref, k_ref, v_ref, u_ref):
    hn = _rmsnorm(x_ref[0], g_ref[...]).astype(BF16)
    proj = jnp.dot(hn, w_ref[...], preferred_element_type=F32)
    scale = HEAD_DIM ** -0.5 * LOG2E
    for p in range(SWA_WIDTH // LANES):
        qp = _pair_rmsnorm(proj[:, p * LANES:(p + 1) * LANES], qg_ref[...])
        q_ref[0, :, p * LANES:(p + 1) * LANES] = (qp * scale).astype(BF16)
    k = _pair_rmsnorm(proj[:, SWA_WIDTH:SWA_WIDTH + KV_WIDTH], kg_ref[...])
    k_ref[0] = k.astype(BF16)
    v_ref[0] = proj[:, SWA_WIDTH + KV_WIDTH:SWA_WIDTH + 2 * KV_WIDTH].astype(BF16)
    u_ref[0] = proj[:, SWA_WIDTH + 2 * KV_WIDTH:]


def _even_in_proj(x, g, w, qg, kg, tm):
    b, l, d = x.shape
    return pl.pallas_call(
        _even_in_kernel,
        grid=(b, l // tm),
        in_specs=[
            pl.BlockSpec((1, tm, d), lambda i, j: (i, j, 0)),
            _const_spec((1, d)),
            _const_spec((d, EVEN_IN)),
            _const_spec((1, LANES)),
            _const_spec((1, LANES)),
        ],
        out_specs=[
            pl.BlockSpec((1, tm, SWA_WIDTH), lambda i, j: (i, j, 0)),
            pl.BlockSpec((1, tm, KV_WIDTH), lambda i, j: (i, j, 0)),
            pl.BlockSpec((1, tm, KV_WIDTH), lambda i, j: (i, j, 0)),
            pl.BlockSpec((1, tm, SSM_WIDTH), lambda i, j: (i, j, 0)),
        ],
        out_shape=[
            jax.ShapeDtypeStruct((b, l, SWA_WIDTH), BF16),
            jax.ShapeDtypeStruct((b, l, KV_WIDTH), BF16),
            jax.ShapeDtypeStruct((b, l, KV_WIDTH), BF16),
            jax.ShapeDtypeStruct((b, l, SSM_WIDTH), F32),
        ],
        compiler_params=_params("parallel", "parallel"),
        name="even_in_proj",
    )(x, g, w, qg, kg)


def _swa_kernel(sink_ref, q_ref, kc_ref, kp_ref, vc_ref, vp_ref, o_ref, *, nsub):
    n = pl.program_id(1)
    blk = SWA_BLOCK
    kall = jnp.concatenate([kp_ref[0], kc_ref[0]], axis=0).astype(F32)
    vall = jnp.concatenate([vp_ref[0], vc_ref[0]], axis=0).astype(F32)
    lo = lax.broadcasted_iota(jnp.int32, kall.shape, 1) < HEAD_DIM

    def halves(t):
        g0_lo = jnp.where(lo, t, 0.0)
        g1_hi = jnp.where(lo, 0.0, t)
        g0_hi = pltpu.roll(g0_lo, HEAD_DIM, 1)
        g1_lo = pltpu.roll(g1_hi, HEAD_DIM, 1)
        return [[g0_lo.astype(BF16), g0_hi.astype(BF16)], [g1_lo.astype(BF16), g1_hi.astype(BF16)]]

    ks, vs = halves(kall), halves(vall)
    qq = lax.broadcasted_iota(jnp.int32, (blk, 2 * blk), 0)
    kk = lax.broadcasted_iota(jnp.int32, (blk, 2 * blk), 1)
    diff = qq + blk - kk
    band = (diff >= 0) & (diff < blk)
    first_key = jnp.where(n > 0, 0, blk)
    band_first = band & (kk >= first_key)

    streams = []
    for s in range(nsub):
        keys = slice(s * blk, (s + 2) * blk)
        for p in range(SWA_WIDTH // LANES):
            g = (2 * p) // (SWA_Q_HEADS // SWA_KV_HEADS)
            for e in range(2):
                streams.append(dict(
                    q=q_ref[0, s * blk:(s + 1) * blk, p * LANES:(p + 1) * LANES],
                    k=ks[g][e][keys], v=vs[g][e][keys],
                    sink=sink_ref[2 * p + e] * LOG2E, mask=band_first if s == 0 else band))
    group = len(streams) // nsub
    for g0 in range(0, len(streams), group):
        for st in streams[g0:g0 + group]:
            st["z"] = lax.dot_general(st["q"], st["k"], (((1,), (1,)), ((), ())),
                                      preferred_element_type=F32)
        for st in streams[g0:g0 + group]:
            z = jnp.where(st["mask"], st.pop("z"), -jnp.inf)
            m = jnp.maximum(jnp.max(z, axis=-1, keepdims=True), st["sink"])
            pe = jnp.exp2(z - m)
            st["den"] = jnp.sum(pe, axis=-1, keepdims=True) + jnp.exp2(st["sink"] - m)
            st["pe"] = pe.astype(BF16)
        for st in streams[g0:g0 + group]:
            st["o"] = jnp.dot(st.pop("pe"), st["v"], preferred_element_type=F32) / st["den"]
    for s in range(nsub):
        for p in range(SWA_WIDTH // LANES):
            i = (s * (SWA_WIDTH // LANES) + p) * 2
            o_ref[0, s * blk:(s + 1) * blk, p * LANES:(p + 1) * LANES] = (
                streams[i]["o"] + streams[i + 1]["o"]).astype(BF16)


def _swa_attention(sinks, q, k, v, nsub):
    b, l, _ = q.shape
    blk = SWA_BLOCK * nsub
    cur = lambda i, j: (i, j, 0)
    prev = lambda i, j: (i, jnp.maximum(j * nsub - 1, 0), 0)
    return pl.pallas_call(
        functools.partial(_swa_kernel, nsub=nsub),
        grid=(b, l // blk),
        in_specs=[
            pl.BlockSpec(memory_space=pltpu.SMEM),
            pl.BlockSpec((1, blk, SWA_WIDTH), cur),
            pl.BlockSpec((1, blk, KV_WIDTH), cur),
            pl.BlockSpec((1, SWA_BLOCK, KV_WIDTH), prev),
            pl.BlockSpec((1, blk, KV_WIDTH), cur),
            pl.BlockSpec((1, SWA_BLOCK, KV_WIDTH), prev),
        ],
        out_specs=pl.BlockSpec((1, blk, SWA_WIDTH), cur),
        out_shape=jax.ShapeDtypeStruct((b, l, SWA_WIDTH), BF16),
        compiler_params=_params("parallel", "parallel"),
        name="swa_attention",
    )(sinks, q, k, k, v, v)


def _ssm_disc_kernel(are_ref, aim_ref, ldt_ref, lre_ref, lim_ref, wre_ref, wim_ref):
    a_re, a_im = are_ref[...], aim_ref[...]
    dt = jnp.exp(ldt_ref[...])
    mag = jnp.exp(a_re * dt)
    lam_re = mag * jnp.cos(a_im * dt)
    lam_im = mag * jnp.sin(a_im * dt)
    den = a_re * a_re + a_im * a_im
    lre_ref[...] = lam_re
    lim_ref[...] = lam_im
    wre_ref[...] = ((lam_re - 1.0) * a_re + lam_im * a_im) / den
    wim_ref[...] = (lam_im * a_re - (lam_re - 1.0) * a_im) / den


def _ssm_bbar_kernel(wre_ref, wim_ref, bre_ref, bim_ref, ore_ref, oim_ref):
    w_re, w_im, b_re, b_im = wre_ref[...], wim_ref[...], bre_ref[...], bim_ref[...]
    ore_ref[...] = w_re * b_re - w_im * b_im
    oim_ref[...] = w_re * b_im + w_im * b_re


def _block_diag(t):
    ch, gl, r, c = t.shape
    eye = jnp.eye(gl, dtype=t.dtype)
    return (t[:, :, :, None, :] * eye[None, :, None, :, None]).reshape(ch, gl * r, gl * c)


def _ssm_prepare(a_re, a_im, b_re, b_im, c_re, c_im, log_dt, batch):
    g, n, p = SSM_GROUPS, SSM_STATE, SSM_GROUP_CH
    gn = jax.ShapeDtypeStruct((g, n), F32)
    ldt = jnp.broadcast_to(log_dt[:, None], (g, n))
    lam_re, lam_im, w_re, w_im = pl.pallas_call(
        _ssm_disc_kernel, out_shape=[gn, gn, gn, gn], name="ssm_discretise")(a_re, a_im, ldt)
    gnp = jax.ShapeDtypeStruct((g, n * p), F32)
    bb_re, bb_im = pl.pallas_call(_ssm_bbar_kernel, out_shape=[gnp, gnp], name="ssm_bbar")(
        jnp.repeat(w_re, p, axis=1), jnp.repeat(w_im, p, axis=1),
        b_re.reshape(g, n * p), b_im.reshape(g, n * p))
    gl = g // SSM_CHUNKS
    to_in = lambda t: _block_diag(
        t.reshape(SSM_CHUNKS, gl, n, p).transpose(0, 1, 3, 2)).astype(BF16)
    to_out = lambda t: _block_diag(
        t.reshape(SSM_CHUNKS, gl, p, n).transpose(0, 1, 3, 2)).astype(BF16)
    bcast = lambda t: jnp.broadcast_to(t.reshape(1, SSM_STATES), (batch, SSM_STATES))
    return (to_in(bb_re), to_in(bb_im), bcast(lam_re), bcast(lam_im), to_out(c_re), to_out(c_im))


def _ssm_kernel(u_ref, bre_ref, bim_ref, lre_ref, lim_ref, cre_ref, cim_ref, d_ref, wg_ref, bg_ref,
                o_ref, ubuf_ref, h0_ref, h1_ref, h2_ref, h3_ref, sre_ref, sim_ref,
                *, batch, steps, pitch):
    @pl.when(pl.program_id(0) == 0)
    def _():
        sre_ref[...] = jnp.zeros_like(sre_ref)
        sim_ref[...] = jnp.zeros_like(sim_ref)
        ubuf_ref[...] = jnp.zeros_like(ubuf_ref)

    for b in range(batch):
        ubuf_ref[b * pitch:b * pitch + steps, :] = u_ref[b]
    u = ubuf_ref[...]
    ub = u.astype(BF16)
    tiles = SSM_CHUNK_STATES // LANES
    ys = [None] * SSM_CHUNKS
    h_refs = (h0_ref, h1_ref, h2_ref, h3_ref)

    def input_matmul(c, part):
        w_ref = (bre_ref, bim_ref)[part]
        bu = jnp.dot(ub[:, c * LANES:(c + 1) * LANES], w_ref[c], preferred_element_type=F32)
        for i in range(tiles):
            h_refs[c][part, i] = bu[:, i * LANES:(i + 1) * LANES]

    def output_matmul(c):
        hs = [jnp.concatenate([h_refs[c][part, i] for i in range(tiles)], axis=1).astype(BF16)
              for part in range(2)]
        ys[c] = (jnp.dot(hs[0], cre_ref[c], preferred_element_type=F32)
                 - jnp.dot(hs[1], cim_ref[c], preferred_element_type=F32))

    def scan(chunks, fillers):
        slabs = [(c, i) for c in chunks for i in range(tiles)]
        lanes = [slice((c * tiles + i) * LANES, (c * tiles + i + 1) * LANES) for c, i in slabs]
        h_re = [sre_ref[:, ln] for ln in lanes]
        h_im = [sim_ref[:, ln] for ln in lanes]
        l_re = [lre_ref[:, ln] for ln in lanes]
        l_im = [lim_ref[:, ln] for ln in lanes]
        fillers = list(fillers)
        every = max(1, steps // max(1, len(fillers)))
        for t in range(steps):
            rows = pl.ds(t, batch, stride=pitch)
            for i, (c, tile) in enumerate(slabs):
                n_re = l_re[i] * h_re[i] - l_im[i] * h_im[i] + h_refs[c][0, tile, rows, :]
                n_im = l_re[i] * h_im[i] + l_im[i] * h_re[i] + h_refs[c][1, tile, rows, :]
                h_refs[c][0, tile, rows, :] = n_re
                h_refs[c][1, tile, rows, :] = n_im
                h_re[i], h_im[i] = n_re, n_im
            if fillers and (t + 1) % every == 0:
                fillers.pop(0)()
        for f in fillers:
            f()
        for i, ln in enumerate(lanes):
            sre_ref[:, ln] = h_re[i]
            sim_ref[:, ln] = h_im[i]

    first, second = (0, 1), (2, 3)
    for c in first:
        input_matmul(c, 0)
        input_matmul(c, 1)
    scan(first, [functools.partial(input_matmul, c, part) for c in second for part in range(2)])
    scan(second, [functools.partial(output_matmul, c) for c in first])
    for c in second:
        output_matmul(c)
    y = jnp.concatenate(ys, axis=-1) + d_ref[...] * u
    y = jax.nn.gelu(y)
    gate = jnp.dot(y.astype(BF16), wg_ref[...], preferred_element_type=F32) + bg_ref[...]
    out = y * jax.nn.sigmoid(gate)
    for b in range(batch):
        o_ref[b] = out[b * pitch:b * pitch + steps].astype(BF16)


def _ssm_mixer(u, mats, d_skip, w_glu, b_glu, steps):
    batch, l, width = u.shape
    assert steps % 8 == 0
    pitch = steps + 4
    b_in_re, b_in_im, lam_re, lam_im, c_out_re, c_out_im = mats
    block = pl.BlockSpec((batch, steps, width), lambda i: (0, i, 0))
    return pl.pallas_call(
        functools.partial(_ssm_kernel, batch=batch, steps=steps, pitch=pitch),
        grid=(l // steps,),
        in_specs=[
            block,
            _const_spec(b_in_re.shape), _const_spec(b_in_im.shape),
            _const_spec(lam_re.shape), _const_spec(lam_im.shape),
            _const_spec(c_out_re.shape), _const_spec(c_out_im.shape),
            _const_spec((1, width)), _const_spec((width, width)), _const_spec((1, width)),
        ],
        out_specs=block,
        out_shape=jax.ShapeDtypeStruct((batch, l, width), BF16),
        scratch_shapes=[
            pltpu.VMEM((batch * pitch, width), F32),
            *[pltpu.VMEM((2, SSM_CHUNK_STATES // LANES, batch * pitch, LANES), F32)
              for _ in range(SSM_CHUNKS)],
            pltpu.VMEM((batch, SSM_STATES), F32), pltpu.VMEM((batch, SSM_STATES), F32),
        ],
        compiler_params=_params("arbitrary"),
        name="ssm_mixer",
    )(u, b_in_re, b_in_im, lam_re, lam_im, c_out_re, c_out_im, d_skip, w_glu, b_glu)


def _odd_in_kernel(x_ref, g_ref, w_ref, q_ref, k_ref, v_ref):
    hn = _rmsnorm(x_ref[0], g_ref[...]).astype(BF16)
    proj = jnp.dot(hn, w_ref[...], preferred_element_type=F32)
    q_ref[0] = (proj[:, :SB_WIDTH] * (HEAD_DIM ** -0.5 * LOG2E)).astype(BF16)
    k_ref[0] = proj[:, SB_WIDTH:2 * SB_WIDTH].astype(BF16)
    v_ref[0] = proj[:, 2 * SB_WIDTH:].astype(BF16)


def _odd_in_proj(x, g, w, tm):
    b, l, d = x.shape
    row = pl.BlockSpec((1, tm, SB_WIDTH), lambda i, j: (i, j, 0))
    out = jax.ShapeDtypeStruct((b, l, SB_WIDTH), BF16)
    return pl.pallas_call(
        _odd_in_kernel,
        grid=(b, l // tm),
        in_specs=[pl.BlockSpec((1, tm, d), lambda i, j: (i, j, 0)),
                  _const_spec((1, d)), _const_spec((d, 3 * SB_WIDTH))],
        out_specs=[row, row, row],
        out_shape=[out, out, out],
        compiler_params=_params("parallel", "parallel"),
        name="odd_in_proj",
    )(x, g, w)


def _sb_streams(streams, tri):
    m = streams[0][0].shape[0]
    half = m // 2
    strict = (lax.broadcasted_iota(jnp.int32, (half, half), 1)
              < lax.broadcasted_iota(jnp.int32, (half, half), 0))

    def lower_triangle(fn, *blocks):
        quad = lambda r, c: fn(*[blk[r * half:(r + 1) * half, c * half:(c + 1) * half]
                                 for blk in blocks])
        top_left = jnp.where(strict, quad(0, 0), 0.0)
        bottom_right = jnp.where(strict, quad(1, 1), 0.0)
        top = jnp.concatenate([top_left, jnp.zeros_like(top_left)], axis=1)
        return jnp.concatenate([top, jnp.concatenate([quad(1, 0), bottom_right], axis=1)], axis=0)

    def softplus2(z):
        return jnp.maximum(z, jnp.log(1.0 + jnp.exp2(jnp.minimum(z, SB_EXP2_CLAMP))) * LOG2E)

    def scores(s):
        s["z"] = lax.dot_general(s["q"], s["k"], (((1,), (1,)), ((), ())),
                                 preferred_element_type=F32)

    def softplus(s):
        sp = lower_triangle(softplus2, s["z"]) if s["diag"] else softplus2(s["z"])
        s["sp"] = sp.astype(BF16)

    def suffix_sums(s):
        s["w"] = jnp.dot(s.pop("sp"), tri, preferred_element_type=F32)

    def weights(s):
        weight = lambda z, w: jnp.exp2(z - w)
        z = s.pop("z")
        p = lower_triangle(weight, z, s["w"]) if s["diag"] else weight(z, s["w"])
        s["p"] = p.astype(BF16)

    def values(s):
        s["pv"] = jnp.dot(s.pop("p"), s["v"], preferred_element_type=F32)

    stages = (scores, softplus, suffix_sums, weights, values)
    state = [dict(q=q, k=kh, v=vb, diag=diag) for q, kh, vb, diag in streams]
    groups = [state[i:i + SB_STREAM_GROUP] for i in range(0, len(state), SB_STREAM_GROUP)]
    for tick in range(len(groups) + len(stages) - 1):
        for g, members in enumerate(groups):
            if 0 <= tick - g < len(stages):
                for s in members:
                    stages[tick - g](s)
    return [(s["pv"], jnp.broadcast_to(s["w"][:, 0:1], (s["pv"].shape[0], LANES))) for s in state]


def _sb_kernel(q_ref, k_ref, v_ref, tri_ref, o_ref, c_ref, acc_ref, *, th, npair):
    qi = pl.program_id(2)
    lo_k = lax.broadcasted_iota(jnp.int32, (th, LANES), 1) < HEAD_DIM
    lo_q = lax.broadcasted_iota(jnp.int32, (th, LANES), 1) < HEAD_DIM
    tri = tri_ref[...]

    def load_kv(pair, j):
        rows = pl.ds(pl.multiple_of(j * th, th), th)
        lanes = slice(pair * LANES, (pair + 1) * LANES)
        kb = k_ref[0, rows, lanes]
        return (jnp.where(lo_k, kb, 0), jnp.where(lo_k, 0, kb)), v_ref[0, rows, lanes]

    def load_q(pair, half):
        return q_ref[0, half * th:(half + 1) * th, pair * LANES:(pair + 1) * LANES]

    j_diag = (2 * qi, 2 * qi + 1)
    has_left = (qi > 0).astype(F32)

    streams = []
    for pair in range(npair):
        kv_left = load_kv(pair, jnp.maximum(2 * qi - 1, 0))
        kv_diag = (load_kv(pair, j_diag[0]), load_kv(pair, j_diag[1]))
        for half in range(2):
            q = load_q(pair, half)
            (kd, vd) = kv_diag[half]
            (kl, vl) = kv_diag[0] if half == 1 else kv_left
            for h in range(2):
                streams += [(q, kd[h], vd, True), (q, kl[h], vl, False)]
    outs = iter(_sb_streams(streams, tri))
    for pair in range(npair):
        for half in range(2):
            for h in range(2):
                (pv_d, c_d), (pv_l, c_l) = next(outs), next(outs)
                scale = jnp.exp2(-c_d)
                if half == 0:
                    scale, c_l = scale * has_left, c_l * has_left
                acc_ref[pair, half, h] = pv_d + pv_l * scale
                c_ref[pair, half, h] = c_d + c_l

    def visit_rest(pair, half):
        q = load_q(pair, half)
        n_rest = jnp.maximum(j_diag[half] - 1, 0)

        def cond(carry):
            i, c_min = carry
            return jnp.logical_and(i < n_rest, c_min < SB_DEAD_LOG2)

        def body(carry):
            i, _ = carry
            ks, vb = load_kv(pair, n_rest - 1 - i)
            outs = _sb_streams([(q, ks[0], vb, False), (q, ks[1], vb, False)], tri)
            for h, (pv, c_blk) in enumerate(outs):
                c = c_ref[pair, half, h]
                acc_ref[pair, half, h] += pv * jnp.exp2(-c)
                c_ref[pair, half, h] = c + c_blk
            return i + 1, jnp.min(c_ref[pair, half])

        lax.while_loop(cond, body, (jnp.int32(0), jnp.min(c_ref[pair, half])))

    @pl.when(jnp.min(c_ref[...]) < SB_DEAD_LOG2)
    def _():
        for pair in range(npair):
            for half in range(2):
                visit_rest(pair, half)

    for pair in range(npair):
        for half in range(2):
            o_ref[0, half * th:(half + 1) * th, pair * LANES:(pair + 1) * LANES] = jnp.where(
                lo_q, acc_ref[pair, half, 0], acc_ref[pair, half, 1]).astype(BF16)


def _sb_attention(q, k, v, th, npair):
    b, l, width = q.shape
    tri = (jnp.arange(th)[:, None] >= jnp.arange(th)[None, :]).astype(BF16)
    tq = 2 * th
    lanes = npair * LANES
    state = pltpu.VMEM((npair, 2, 2, th, LANES), F32)
    return pl.pallas_call(
        functools.partial(_sb_kernel, th=th, npair=npair),
        grid=(b, width // lanes, l // tq),
        in_specs=[
            pl.BlockSpec((1, tq, lanes), lambda i, h, j: (i, j, h)),
            pl.BlockSpec((1, l, lanes), lambda i, h, j: (i, 0, h)),
            pl.BlockSpec((1, l, lanes), lambda i, h, j: (i, 0, h)),
            _const_spec((th, th)),
        ],
        out_specs=pl.BlockSpec((1, tq, lanes), lambda i, h, j: (i, j, h)),
        out_shape=jax.ShapeDtypeStruct((b, l, width), BF16),
        scratch_shapes=[state, state],
        compiler_params=_params("parallel", "parallel", "arbitrary"),
        name="sb_attention",
    )(q, k, v, tri)


def _post_kernel(*refs, n_mix):
    x_ref = refs[0]
    mix_refs = refs[1:1 + n_mix]
    wo_refs = refs[1 + n_mix:1 + 2 * n_mix]
    g_ref, wg_ref, wu_ref, wd_ref, o_ref = refs[1 + 2 * n_mix:]
    x = x_ref[0]
    for m_ref, w_ref in zip(mix_refs, wo_refs):
        x = x + jnp.dot(m_ref[0], w_ref[...], preferred_element_type=F32)
    hn = _rmsnorm(x, g_ref[...]).astype(BF16)
    gate = jnp.dot(hn, wg_ref[...], preferred_element_type=F32)
    up = jnp.dot(hn, wu_ref[...], preferred_element_type=F32)
    act = (gate * jax.nn.sigmoid(gate) * up).astype(BF16)
    o_ref[0] = x + jnp.dot(act, wd_ref[...], preferred_element_type=F32)


def _post(x, mixes, mix_specs, w_outs, g, w_gate, w_up, w_down, tm):
    b, l, d = x.shape
    xspec = pl.BlockSpec((1, tm, d), lambda i, j: (i, j, 0))
    return pl.pallas_call(
        functools.partial(_post_kernel, n_mix=len(mixes)),
        grid=(b, l // tm),
        in_specs=[xspec, *mix_specs, *[_const_spec(w.shape) for w in w_outs],
                  _const_spec((1, d)), _const_spec(w_gate.shape), _const_spec(w_up.shape),
                  _const_spec(w_down.shape)],
        out_specs=xspec,
        out_shape=jax.ShapeDtypeStruct(x.shape, F32),
        compiler_params=_params("parallel", "parallel"),
        name="out_proj_ffn",
    )(x, *mixes, *w_outs, g, w_gate, w_up, w_down)


def _tiles(l):
    tm = min(512, l)
    th = min(256, l // 2)
    steps = min(64, l)
    nsub = min(4, l // SWA_BLOCK)
    return tm, th, steps, nsub


def kernel(x, even_norm, even_w_in, q_norm, k_norm, sinks, ssm_a_re, ssm_a_im, ssm_b_re, ssm_b_im,
           ssm_c_re, ssm_c_im, ssm_d, ssm_log_dt, ssm_w_glu, ssm_b_glu, even_w_out, odd_norm,
           odd_w_in, odd_w_out, ffn_norm, ffn_w_gate, ffn_w_up, ffn_w_down):
    b, l, d = x.shape
    assert d == D_MODEL and l % SWA_BLOCK == 0
    tm, th, steps, nsub = _tiles(l)
    depth = ffn_norm.shape[0]
    bf = lambda t: t.astype(BF16)
    row = lambda t: t.reshape(1, -1)
    for layer in range(depth):
        i = layer // 2
        if layer % 2 == 0:
            q, k, v, u = _even_in_proj(
                x, row(even_norm[i]), bf(even_w_in[i]),
                row(jnp.tile(q_norm[i], 2)), row(jnp.tile(k_norm[i], 2)), tm)
            o_attn = _swa_attention(sinks[i], q, k, v, nsub)
            mats = _ssm_prepare(ssm_a_re[i], ssm_a_im[i], ssm_b_re[i], ssm_b_im[i],
                                ssm_c_re[i], ssm_c_im[i], ssm_log_dt[i], b)
            o_ssm = _ssm_mixer(u, mats, row(ssm_d[i]), bf(ssm_w_glu[i]), row(ssm_b_glu[i]), steps)
            mixes = [o_attn, o_ssm]
            mix_specs = [pl.BlockSpec((1, tm, SWA_WIDTH), lambda bi, j: (bi, j, 0)),
                         pl.BlockSpec((1, tm, SSM_WIDTH), lambda bi, j: (bi, j, 0))]
            w_outs = [bf(even_w_out[i][:SWA_WIDTH]), bf(even_w_out[i][SWA_WIDTH:])]
        else:
            q, k, v = _odd_in_proj(x, row(odd_norm[i]), bf(odd_w_in[i]), tm)
            mixes = [_sb_attention(q, k, v, th, SB_PAIRS_PER_STEP)]
            mix_specs = [pl.BlockSpec((1, tm, SB_WIDTH), lambda bi, j: (bi, j, 0))]
            w_outs = [bf(odd_w_out[i])]
        x = _post(x, mixes, mix_specs, w_outs, row(ffn_norm[layer]), bf(ffn_w_gate[layer]),
                  bf(ffn_w_up[layer]), bf(ffn_w_down[layer]), tm)
    return x
```

```python
import functools
import math

import jax
import jax.numpy as jnp
from jax import lax
from jax.experimental import pallas as pl
from jax.experimental.pallas import tpu as pltpu

F32 = jnp.float32
BF16 = jnp.bfloat16

D_MODEL = 1024
HEAD_DIM = 64
EPS = 1e-6
LANES = 128

SWA_Q_HEADS = 8
SWA_KV_HEADS = 2
SWA_BLOCK = 128
SWA_WIDTH = SWA_Q_HEADS * HEAD_DIM
KV_WIDTH = SWA_KV_HEADS * HEAD_DIM

SSM_WIDTH = D_MODEL // 2
SSM_GROUP_CH = 16
SSM_GROUPS = SSM_WIDTH // SSM_GROUP_CH
SSM_STATE = 64
SSM_STATES = SSM_GROUPS * SSM_STATE
SSM_CHUNKS = SSM_WIDTH // LANES
SSM_CHUNK_STATES = SSM_STATES // SSM_CHUNKS
EVEN_IN = SWA_WIDTH + 2 * KV_WIDTH + SSM_WIDTH

SB_HEADS = D_MODEL // HEAD_DIM
SB_WIDTH = SB_HEADS * HEAD_DIM
LOG2E = math.log2(math.e)
SB_DEAD_LOG2 = 160.0
SB_EXP2_CLAMP = 126.0
SB_PAIRS_PER_STEP = 2
SB_STREAM_GROUP = 2

VMEM_LIMIT_BYTES = 56 * 1024 * 1024
ROW_SPLIT = 2


def _params(*sem):
    return pltpu.CompilerParams(dimension_semantics=sem, vmem_limit_bytes=VMEM_LIMIT_BYTES)


def _const_spec(shape):
    nd = len(shape)
    return pl.BlockSpec(shape, lambda *_: (0,) * nd, pipeline_mode=pl.Buffered(1))


def _row_subblocks(tm):
    step = tm // ROW_SPLIT
    return [slice(i * step, (i + 1) * step) for i in range(ROW_SPLIT)]


def _rmsnorm(x, g):
    ms = jnp.mean(x * x, axis=-1, keepdims=True)
    return x * lax.rsqrt(ms + EPS) * g


def _pair_rmsnorm(x, g2):
    lo = lax.broadcasted_iota(jnp.int32, x.shape, 1) < HEAD_DIM
    sq = x * x
    s_lo = jnp.sum(jnp.where(lo, sq, 0.0), axis=-1, keepdims=True)
    s_hi = jnp.sum(jnp.where(lo, 0.0, sq), axis=-1, keepdims=True)
    ms = jnp.where(lo, s_lo, s_hi) * (1.0 / HEAD_DIM)
    return x * lax.rsqrt(ms + EPS) * g2


def _even_in_kernel(x_ref, g_ref, w_ref, qg_ref, kg_ref, q_ref, k_ref, v_ref, u_ref):
    subs = _row_subblocks(x_ref.shape[1])
    hns = [_rmsnorm(x_ref[0, rows, :], g_ref[...]).astype(BF16) for rows in subs]
    projs = [jnp.dot(hn, w_ref[...], preferred_element_type=F32) for hn in hns]
    scale = HEAD_DIM ** -0.5 * LOG2E
    for proj, rows in zip(projs, subs):
        for p in range(SWA_WIDTH // LANES):
            qp = _pair_rmsnorm(proj[:, p * LANES:(p + 1) * LANES], qg_ref[...])
            q_ref[0, rows, p * LANES:(p + 1) * LANES] = (qp * scale).astype(BF16)
        k = _pair_rmsnorm(proj[:, SWA_WIDTH:SWA_WIDTH + KV_WIDTH], kg_ref[...])
        k_ref[0, rows, :] = k.astype(BF16)
        v_ref[0, rows, :] = proj[:, SWA_WIDTH + KV_WIDTH:SWA_WIDTH + 2 * KV_WIDTH].astype(BF16)
        u_ref[0, rows, :] = proj[:, SWA_WIDTH + 2 * KV_WIDTH:]


def _even_in_proj(x, g, w, qg, kg, tm):
    b, l, d = x.shape
    return pl.pallas_call(
        _even_in_kernel,
        grid=(b, l // tm),
        in_specs=[
            pl.BlockSpec((1, tm, d), lambda i, j: (i, j, 0)),
            _const_spec((1, d)),
            _const_spec((d, EVEN_IN)),
            _const_spec((1, LANES)),
            _const_spec((1, LANES)),
        ],
        out_specs=[
            pl.BlockSpec((1, tm, SWA_WIDTH), lambda i, j: (i, j, 0)),
            pl.BlockSpec((1, tm, KV_WIDTH), lambda i, j: (i, j, 0)),
            pl.BlockSpec((1, tm, KV_WIDTH), lambda i, j: (i, j, 0)),
            pl.BlockSpec((1, tm, SSM_WIDTH), lambda i, j: (i, j, 0)),
        ],
        out_shape=[
            jax.ShapeDtypeStruct((b, l, SWA_WIDTH), BF16),
            jax.ShapeDtypeStruct((b, l, KV_WIDTH), BF16),
            jax.ShapeDtypeStruct((b, l, KV_WIDTH), BF16),
            jax.ShapeDtypeStruct((b, l, SSM_WIDTH), F32),
        ],
        compiler_params=_params("parallel", "parallel"),
        name="even_in_proj",
    )(x, g, w, qg, kg)


def _swa_kernel(sink_ref, q_ref, kc_ref, kp_ref, vc_ref, vp_ref, o_ref, *, nsub):
    n = pl.program_id(1)
    blk = SWA_BLOCK
    kall = jnp.concatenate([kp_ref[0], kc_ref[0]], axis=0).astype(F32)
    vall = jnp.concatenate([vp_ref[0], vc_ref[0]], axis=0).astype(F32)
    lo = lax.broadcasted_iota(jnp.int32, kall.shape, 1) < HEAD_DIM

    def halves(t):
        g0_lo = jnp.where(lo, t, 0.0)
        g1_hi = jnp.where(lo, 0.0, t)
        g0_hi = pltpu.roll(g0_lo, HEAD_DIM, 1)
        g1_lo = pltpu.roll(g1_hi, HEAD_DIM, 1)
        return [[g0_lo.astype(BF16), g0_hi.astype(BF16)], [g1_lo.astype(BF16), g1_hi.astype(BF16)]]

    ks, vs = halves(kall), halves(vall)
    qq = lax.broadcasted_iota(jnp.int32, (blk, 2 * blk), 0)
    kk = lax.broadcasted_iota(jnp.int32, (blk, 2 * blk), 1)
    diff = qq + blk - kk
    band = (diff >= 0) & (diff < blk)
    first_key = jnp.where(n > 0, 0, blk)
    band_first = band & (kk >= first_key)

    streams = []
    for s in range(nsub):
        keys = slice(s * blk, (s + 2) * blk)
        for p in range(SWA_WIDTH // LANES):
            g = (2 * p) // (SWA_Q_HEADS // SWA_KV_HEADS)
            for e in range(2):
                streams.append(dict(
                    q=q_ref[0, s * blk:(s + 1) * blk, p * LANES:(p + 1) * LANES],
                    k=ks[g][e][keys], v=vs[g][e][keys],
                    sink=sink_ref[2 * p + e] * LOG2E, mask=band_first if s == 0 else band))
    group = len(streams) // nsub
    for g0 in range(0, len(streams), group):
        for st in streams[g0:g0 + group]:
            st["z"] = lax.dot_general(st["q"], st["k"], (((1,), (1,)), ((), ())),
                                      preferred_element_type=F32)
        for st in streams[g0:g0 + group]:
            z = jnp.where(st["mask"], st.pop("z"), -jnp.inf)
            m = jnp.maximum(jnp.max(z, axis=-1, keepdims=True), st["sink"])
            pe = jnp.exp2(z - m)
            st["den"] = jnp.sum(pe, axis=-1, keepdims=True) + jnp.exp2(st["sink"] - m)
            st["pe"] = pe.astype(BF16)
        for st in streams[g0:g0 + group]:
            st["o"] = jnp.dot(st.pop("pe"), st["v"], preferred_element_type=F32) / st["den"]
    for s in range(nsub):
        for p in range(SWA_WIDTH // LANES):
            i = (s * (SWA_WIDTH // LANES) + p) * 2
            o_ref[0, s * blk:(s + 1) * blk, p * LANES:(p + 1) * LANES] = (
                streams[i]["o"] + streams[i + 1]["o"]).astype(BF16)


def _swa_attention(sinks, q, k, v, nsub):
    b, l, _ = q.shape
    blk = SWA_BLOCK * nsub
    cur = lambda i, j: (i, j, 0)
    prev = lambda i, j: (i, jnp.maximum(j * nsub - 1, 0), 0)
    return pl.pallas_call(
        functools.partial(_swa_kernel, nsub=nsub),
        grid=(b, l // blk),
        in_specs=[
            pl.BlockSpec(memory_space=pltpu.SMEM),
            pl.BlockSpec((1, blk, SWA_WIDTH), cur),
            pl.BlockSpec((1, blk, KV_WIDTH), cur),
            pl.BlockSpec((1, SWA_BLOCK, KV_WIDTH), prev),
            pl.BlockSpec((1, blk, KV_WIDTH), cur),
            pl.BlockSpec((1, SWA_BLOCK, KV_WIDTH), prev),
        ],
        out_specs=pl.BlockSpec((1, blk, SWA_WIDTH), cur),
        out_shape=jax.ShapeDtypeStruct((b, l, SWA_WIDTH), BF16),
        compiler_params=_params("parallel", "parallel"),
        name="swa_attention",
    )(sinks, q, k, k, v, v)


def _ssm_disc_kernel(are_ref, aim_ref, ldt_ref, lre_ref, lim_ref, wre_ref, wim_ref):
    a_re, a_im = are_ref[...], aim_ref[...]
    dt = jnp.exp(ldt_ref[...])
    mag = jnp.exp(a_re * dt)
    lam_re = mag * jnp.cos(a_im * dt)
    lam_im = mag * jnp.sin(a_im * dt)
    den = a_re * a_re + a_im * a_im
    lre_ref[...] = lam_re
    lim_ref[...] = lam_im
    wre_ref[...] = ((lam_re - 1.0) * a_re + lam_im * a_im) / den
    wim_ref[...] = (lam_im * a_re - (lam_re - 1.0) * a_im) / den


def _ssm_bbar_kernel(wre_ref, wim_ref, bre_ref, bim_ref, ore_ref, oim_ref):
    w_re, w_im, b_re, b_im = wre_ref[...], wim_ref[...], bre_ref[...], bim_ref[...]
    ore_ref[...] = w_re * b_re - w_im * b_im
    oim_ref[...] = w_re * b_im + w_im * b_re


def _block_diag(t):
    ch, gl, r, c = t.shape
    eye = jnp.eye(gl, dtype=t.dtype)
    return (t[:, :, :, None, :] * eye[None, :, None, :, None]).reshape(ch, gl * r, gl * c)


def _ssm_prepare(a_re, a_im, b_re, b_im, c_re, c_im, log_dt, batch):
    g, n, p = SSM_GROUPS, SSM_STATE, SSM_GROUP_CH
    gn = jax.ShapeDtypeStruct((g, n), F32)
    ldt = jnp.broadcast_to(log_dt[:, None], (g, n))
    lam_re, lam_im, w_re, w_im = pl.pallas_call(
        _ssm_disc_kernel, out_shape=[gn, gn, gn, gn], name="ssm_discretise")(a_re, a_im, ldt)
    gnp = jax.ShapeDtypeStruct((g, n * p), F32)
    bb_re, bb_im = pl.pallas_call(_ssm_bbar_kernel, out_shape=[gnp, gnp], name="ssm_bbar")(
        jnp.repeat(w_re, p, axis=1), jnp.repeat(w_im, p, axis=1),
        b_re.reshape(g, n * p), b_im.reshape(g, n * p))
    gl = g // SSM_CHUNKS
    to_in = lambda t: _block_diag(
        t.reshape(SSM_CHUNKS, gl, n, p).transpose(0, 1, 3, 2)).astype(BF16)
    to_out = lambda t: _block_diag(
        t.reshape(SSM_CHUNKS, gl, p, n).transpose(0, 1, 3, 2)).astype(BF16)
    bcast = lambda t: jnp.broadcast_to(t.reshape(1, SSM_STATES), (batch, SSM_STATES))
    return (to_in(bb_re), to_in(bb_im), bcast(lam_re), bcast(lam_im), to_out(c_re), to_out(c_im))


def _ssm_kernel(u_ref, bre_ref, bim_ref, lre_ref, lim_ref, cre_ref, cim_ref, d_ref, wg_ref, bg_ref,
                o_ref, ubuf_ref, h0_ref, h1_ref, h2_ref, h3_ref, sre_ref, sim_ref,
                *, batch, steps, pitch):
    @pl.when(pl.program_id(0) == 0)
    def _():
        sre_ref[...] = jnp.zeros_like(sre_ref)
        sim_ref[...] = jnp.zeros_like(sim_ref)
        ubuf_ref[...] = jnp.zeros_like(ubuf_ref)

    for b in range(batch):
        ubuf_ref[b * pitch:b * pitch + steps, :] = u_ref[b]
    u = ubuf_ref[...]
    ub = u.astype(BF16)
    tiles = SSM_CHUNK_STATES // LANES
    ys = [None] * SSM_CHUNKS
    h_refs = (h0_ref, h1_ref, h2_ref, h3_ref)

    def input_matmul(c, part):
        w_ref = (bre_ref, bim_ref)[part]
        bu = jnp.dot(ub[:, c * LANES:(c + 1) * LANES], w_ref[c], preferred_element_type=F32)
        for i in range(tiles):
            h_refs[c][part, i] = bu[:, i * LANES:(i + 1) * LANES]

    def output_matmul(c):
        hs = [jnp.concatenate([h_refs[c][part, i] for i in range(tiles)], axis=1).astype(BF16)
              for part in range(2)]
        ys[c] = (jnp.dot(hs[0], cre_ref[c], preferred_element_type=F32)
                 - jnp.dot(hs[1], cim_ref[c], preferred_element_type=F32))

    def scan(chunks, fillers):
        slabs = [(c, i) for c in chunks for i in range(tiles)]
        lanes = [slice((c * tiles + i) * LANES, (c * tiles + i + 1) * LANES) for c, i in slabs]
        h_re = [sre_ref[:, ln] for ln in lanes]
        h_im = [sim_ref[:, ln] for ln in lanes]
        l_re = [lre_ref[:, ln] for ln in lanes]
        l_im = [lim_ref[:, ln] for ln in lanes]
        fillers = list(fillers)
        every = max(1, steps // max(1, len(fillers)))
        for t in range(steps):
            rows = pl.ds(t, batch, stride=pitch)
            for i, (c, tile) in enumerate(slabs):
                n_re = l_re[i] * h_re[i] - l_im[i] * h_im[i] + h_refs[c][0, tile, rows, :]
                n_im = l_re[i] * h_im[i] + l_im[i] * h_re[i] + h_refs[c][1, tile, rows, :]
                h_refs[c][0, tile, rows, :] = n_re
                h_refs[c][1, tile, rows, :] = n_im
                h_re[i], h_im[i] = n_re, n_im
            if fillers and (t + 1) % every == 0:
                fillers.pop(0)()
        for f in fillers:
            f()
        for i, ln in enumerate(lanes):
            sre_ref[:, ln] = h_re[i]
            sim_ref[:, ln] = h_im[i]

    first, second = (0, 1), (2, 3)
    for c in first:
        input_matmul(c, 0)
        input_matmul(c, 1)
    scan(first, [functools.partial(input_matmul, c, part) for c in second for part in range(2)])
    scan(second, [functools.partial(output_matmul, c) for c in first])
    for c in second:
        output_matmul(c)
    y = jnp.concatenate(ys, axis=-1) + d_ref[...] * u
    y = jax.nn.gelu(y)
    gate = jnp.dot(y.astype(BF16), wg_ref[...], preferred_element_type=F32) + bg_ref[...]
    out = y * jax.nn.sigmoid(gate)
    for b in range(batch):
        o_ref[b] = out[b * pitch:b * pitch + steps].astype(BF16)


def _ssm_mixer(u, mats, d_skip, w_glu, b_glu, steps):
    batch, l, width = u.shape
    assert steps % 8 == 0
    pitch = steps + 4
    b_in_re, b_in_im, lam_re, lam_im, c_out_re, c_out_im = mats
    block = pl.BlockSpec((batch, steps, width), lambda i: (0, i, 0))
    return pl.pallas_call(
        functools.partial(_ssm_kernel, batch=batch, steps=steps, pitch=pitch),
        grid=(l // steps,),
        in_specs=[
            block,
            _const_spec(b_in_re.shape), _const_spec(b_in_im.shape),
            _const_spec(lam_re.shape), _const_spec(lam_im.shape),
            _const_spec(c_out_re.shape), _const_spec(c_out_im.shape),
            _const_spec((1, width)), _const_spec((width, width)), _const_spec((1, width)),
        ],
        out_specs=block,
        out_shape=jax.ShapeDtypeStruct((batch, l, width), BF16),
        scratch_shapes=[
            pltpu.VMEM((batch * pitch, width), F32),
            *[pltpu.VMEM((2, SSM_CHUNK_STATES // LANES, batch * pitch, LANES), F32)
              for _ in range(SSM_CHUNKS)],
            pltpu.VMEM((batch, SSM_STATES), F32), pltpu.VMEM((batch, SSM_STATES), F32),
        ],
        compiler_params=_params("arbitrary"),
        name="ssm_mixer",
    )(u, b_in_re, b_in_im, lam_re, lam_im, c_out_re, c_out_im, d_skip, w_glu, b_glu)


def _odd_in_kernel(x_ref, g_ref, w_ref, q_ref, k_ref, v_ref):
    subs = _row_subblocks(x_ref.shape[1])
    hns = [_rmsnorm(x_ref[0, rows, :], g_ref[...]).astype(BF16) for rows in subs]
    projs = [jnp.dot(hn, w_ref[...], preferred_element_type=F32) for hn in hns]
    for proj, rows in zip(projs, subs):
        q_ref[0, rows, :] = (proj[:, :SB_WIDTH] * (HEAD_DIM ** -0.5 * LOG2E)).astype(BF16)
        k_ref[0, rows, :] = proj[:, SB_WIDTH:2 * SB_WIDTH].astype(BF16)
        v_ref[0, rows, :] = proj[:, 2 * SB_WIDTH:].astype(BF16)


def _odd_in_proj(x, g, w, tm):
    b, l, d = x.shape
    row = pl.BlockSpec((1, tm, SB_WIDTH), lambda i, j: (i, j, 0))
    out = jax.ShapeDtypeStruct((b, l, SB_WIDTH), BF16)
    return pl.pallas_call(
        _odd_in_kernel,
        grid=(b, l // tm),
        in_specs=[pl.BlockSpec((1, tm, d), lambda i, j: (i, j, 0)),
                  _const_spec((1, d)), _const_spec((d, 3 * SB_WIDTH))],
        out_specs=[row, row, row],
        out_shape=[out, out, out],
        compiler_params=_params("parallel", "parallel"),
        name="odd_in_proj",
    )(x, g, w)


def _sb_streams(streams, tri):
    m = streams[0][0].shape[0]
    half = m // 2
    strict = (lax.broadcasted_iota(jnp.int32, (half, half), 1)
              < lax.broadcasted_iota(jnp.int32, (half, half), 0))

    def lower_triangle(fn, *blocks):
        quad = lambda r, c: fn(*[blk[r * half:(r + 1) * half, c * half:(c + 1) * half]
                                 for blk in blocks])
        top_left = jnp.where(strict, quad(0, 0), 0.0)
        bottom_right = jnp.where(strict, quad(1, 1), 0.0)
        top = jnp.concatenate([top_left, jnp.zeros_like(top_left)], axis=1)
        return jnp.concatenate([top, jnp.concatenate([quad(1, 0), bottom_right], axis=1)], axis=0)

    def softplus2(z):
        return jnp.maximum(z, jnp.log(1.0 + jnp.exp2(jnp.minimum(z, SB_EXP2_CLAMP))) * LOG2E)

    def scores(s):
        s["z"] = lax.dot_general(s["q"], s["k"], (((1,), (1,)), ((), ())),
                                 preferred_element_type=F32)

    def softplus(s):
        sp = lower_triangle(softplus2, s["z"]) if s["diag"] else softplus2(s["z"])
        s["sp"] = sp.astype(BF16)

    def suffix_sums(s):
        s["w"] = jnp.dot(s.pop("sp"), tri, preferred_element_type=F32)

    def weights(s):
        weight = lambda z, w: jnp.exp2(z - w)
        z = s.pop("z")
        p = lower_triangle(weight, z, s["w"]) if s["diag"] else weight(z, s["w"])
        s["p"] = p.astype(BF16)

    def values(s):
        s["pv"] = jnp.dot(s.pop("p"), s["v"], preferred_element_type=F32)

    stages = (scores, softplus, suffix_sums, weights, values)
    state = [dict(q=q, k=kh, v=vb, diag=diag) for q, kh, vb, diag in streams]
    groups = [state[i:i + SB_STREAM_GROUP] for i in range(0, len(state), SB_STREAM_GROUP)]
    for tick in range(len(groups) + len(stages) - 1):
        for g, members in enumerate(groups):
            if 0 <= tick - g < len(stages):
                for s in members:
                    stages[tick - g](s)
    return [(s["pv"], jnp.broadcast_to(s["w"][:, 0:1], (s["pv"].shape[0], LANES))) for s in state]


def _sb_kernel(q_ref, k_ref, v_ref, tri_ref, o_ref, c_ref, acc_ref, *, th, npair):
    qi = pl.program_id(2)
    lo_k = lax.broadcasted_iota(jnp.int32, (th, LANES), 1) < HEAD_DIM
    lo_q = lax.broadcasted_iota(jnp.int32, (th, LANES), 1) < HEAD_DIM
    tri = tri_ref[...]

    def load_kv(pair, j):
        rows = pl.ds(pl.multiple_of(j * th, th), th)
        lanes = slice(pair * LANES, (pair + 1) * LANES)
        kb = k_ref[0, rows, lanes]
        return (jnp.where(lo_k, kb, 0), jnp.where(lo_k, 0, kb)), v_ref[0, rows, lanes]

    def load_q(pair, half):
        return q_ref[0, half * th:(half + 1) * th, pair * LANES:(pair + 1) * LANES]

    j_diag = (2 * qi, 2 * qi + 1)
    has_left = (qi > 0).astype(F32)

    streams = []
    for pair in range(npair):
        kv_left = load_kv(pair, jnp.maximum(2 * qi - 1, 0))
        kv_diag = (load_kv(pair, j_diag[0]), load_kv(pair, j_diag[1]))
        for half in range(2):
            q = load_q(pair, half)
            (kd, vd) = kv_diag[half]
            (kl, vl) = kv_diag[0] if half == 1 else kv_left
            for h in range(2):
                streams += [(q, kd[h], vd, True), (q, kl[h], vl, False)]
    outs = iter(_sb_streams(streams, tri))
    for pair in range(npair):
        for half in range(2):
            for h in range(2):
                (pv_d, c_d), (pv_l, c_l) = next(outs), next(outs)
                scale = jnp.exp2(-c_d)
                if half == 0:
                    scale, c_l = scale * has_left, c_l * has_left
                acc_ref[pair, half, h] = pv_d + pv_l * scale
                c_ref[pair, half, h] = c_d + c_l

    def visit_rest(pair, half):
        q = load_q(pair, half)
        n_rest = jnp.maximum(j_diag[half] - 1, 0)

        def cond(carry):
            i, c_min = carry
            return jnp.logical_and(i < n_rest, c_min < SB_DEAD_LOG2)

        def body(carry):
            i, _ = carry
            ks, vb = load_kv(pair, n_rest - 1 - i)
            outs = _sb_streams([(q, ks[0], vb, False), (q, ks[1], vb, False)], tri)
            for h, (pv, c_blk) in enumerate(outs):
                c = c_ref[pair, half, h]
                acc_ref[pair, half, h] += pv * jnp.exp2(-c)
                c_ref[pair, half, h] = c + c_blk
            return i + 1, jnp.min(c_ref[pair, half])

        lax.while_loop(cond, body, (jnp.int32(0), jnp.min(c_ref[pair, half])))

    @pl.when(jnp.min(c_ref[...]) < SB_DEAD_LOG2)
    def _():
        for pair in range(npair):
            for half in range(2):
                visit_rest(pair, half)

    for pair in range(npair):
        for half in range(2):
            o_ref[0, half * th:(half + 1) * th, pair * LANES:(pair + 1) * LANES] = jnp.where(
                lo_q, acc_ref[pair, half, 0], acc_ref[pair, half, 1]).astype(BF16)


def _sb_attention(q, k, v, th, npair):
    b, l, width = q.shape
    tri = (jnp.arange(th)[:, None] >= jnp.arange(th)[None, :]).astype(BF16)
    tq = 2 * th
    lanes = npair * LANES
    state = pltpu.VMEM((npair, 2, 2, th, LANES), F32)
    return pl.pallas_call(
        functools.partial(_sb_kernel, th=th, npair=npair),
        grid=(b, width // lanes, l // tq),
        in_specs=[
            pl.BlockSpec((1, tq, lanes), lambda i, h, j: (i, j, h)),
            pl.BlockSpec((1, l, lanes), lambda i, h, j: (i, 0, h)),
            pl.BlockSpec((1, l, lanes), lambda i, h, j: (i, 0, h)),
            _const_spec((th, th)),
        ],
        out_specs=pl.BlockSpec((1, tq, lanes), lambda i, h, j: (i, j, h)),
        out_shape=jax.ShapeDtypeStruct((b, l, width), BF16),
        scratch_shapes=[state, state],
        compiler_params=_params("parallel", "parallel", "arbitrary"),
        name="sb_attention",
    )(q, k, v, tri)


def _post_kernel(*refs, n_mix):
    x_ref = refs[0]
    mix_refs = refs[1:1 + n_mix]
    wo_refs = refs[1 + n_mix:1 + 2 * n_mix]
    g_ref, wg_ref, wu_ref, wd_ref, o_ref = refs[1 + 2 * n_mix:]
    subs = _row_subblocks(x_ref.shape[1])
    xs = [x_ref[0, rows, :] for rows in subs]
    for m_ref, w_ref in zip(mix_refs, wo_refs):
        xs = [x + jnp.dot(m_ref[0, rows, :], w_ref[...], preferred_element_type=F32)
              for x, rows in zip(xs, subs)]
    hns = [_rmsnorm(x, g_ref[...]).astype(BF16) for x in xs]
    gates = [jnp.dot(hn, wg_ref[...], preferred_element_type=F32) for hn in hns]
    ups = [jnp.dot(hn, wu_ref[...], preferred_element_type=F32) for hn in hns]
    acts = [(gate * jax.nn.sigmoid(gate) * up).astype(BF16) for gate, up in zip(gates, ups)]
    for x, act, rows in zip(xs, acts, subs):
        o_ref[0, rows, :] = x + jnp.dot(act, wd_ref[...], preferred_element_type=F32)


def _post(x, mixes, mix_specs, w_outs, g, w_gate, w_up, w_down, tm):
    b, l, d = x.shape
    xspec = pl.BlockSpec((1, tm, d), lambda i, j: (i, j, 0))
    return pl.pallas_call(
        functools.partial(_post_kernel, n_mix=len(mixes)),
        grid=(b, l // tm),
        in_specs=[xspec, *mix_specs, *[_const_spec(w.shape) for w in w_outs],
                  _const_spec((1, d)), _const_spec(w_gate.shape), _const_spec(w_up.shape),
                  _const_spec(w_down.shape)],
        out_specs=xspec,
        out_shape=jax.ShapeDtypeStruct(x.shape, F32),
        compiler_params=_params("parallel", "parallel"),
        name="out_proj_ffn",
    )(x, *mixes, *w_outs, g, w_gate, w_up, w_down)


def _tiles(l):
    tm = min(512, l)
    th = min(256, l // 2)
    steps = min(64, l)
    nsub = min(4, l // SWA_BLOCK)
    return tm, th, steps, nsub


def kernel(x, even_norm, even_w_in, q_norm, k_norm, sinks, ssm_a_re, ssm_a_im, ssm_b_re, ssm_b_im,
           ssm_c_re, ssm_c_im, ssm_d, ssm_log_dt, ssm_w_glu, ssm_b_glu, even_w_out, odd_norm,
           odd_w_in, odd_w_out, ffn_norm, ffn_w_gate, ffn_w_up, ffn_w_down):
    b, l, d = x.shape
    assert d == D_MODEL and l % SWA_BLOCK == 0
    tm, th, steps, nsub = _tiles(l)
    depth = ffn_norm.shape[0]
    bf = lambda t: t.astype(BF16)
    row = lambda t: t.reshape(1, -1)
    for layer in range(depth):
        i = layer // 2
        if layer % 2 == 0:
            q, k, v, u = _even_in_proj(
                x, row(even_norm[i]), bf(even_w_in[i]),
                row(jnp.tile(q_norm[i], 2)), row(jnp.tile(k_norm[i], 2)), tm)
            o_attn = _swa_attention(sinks[i], q, k, v, nsub)
            mats = _ssm_prepare(ssm_a_re[i], ssm_a_im[i], ssm_b_re[i], ssm_b_im[i],
                                ssm_c_re[i], ssm_c_im[i], ssm_log_dt[i], b)
            o_ssm = _ssm_mixer(u, mats, row(ssm_d[i]), bf(ssm_w_glu[i]), row(ssm_b_glu[i]), steps)
            mixes = [o_attn, o_ssm]
            mix_specs = [pl.BlockSpec((1, tm, SWA_WIDTH), lambda bi, j: (bi, j, 0)),
                         pl.BlockSpec((1, tm, SSM_WIDTH), lambda bi, j: (bi, j, 0))]
            w_outs = [bf(even_w_out[i][:SWA_WIDTH]), bf(even_w_out[i][SWA_WIDTH:])]
        else:
            q, k, v = _odd_in_proj(x, row(odd_norm[i]), bf(odd_w_in[i]), tm)
            mixes = [_sb_attention(q, k, v, th, SB_PAIRS_PER_STEP)]
            mix_specs = [pl.BlockSpec((1, tm, SB_WIDTH), lambda bi, j: (bi, j, 0))]
            w_outs = [bf(odd_w_out[i])]
        x = _post(x, mixes, mix_specs, w_outs, row(ffn_norm[layer]), bf(ffn_w_gate[layer]),
                  bf(ffn_w_up[layer]), bf(ffn_w_down[layer]), tm)
    return x
```

```python
import functools
import math

import jax
import jax.numpy as jnp
from jax import lax
from jax.experimental import pallas as pl
from jax.experimental.pallas import tpu as pltpu

F32 = jnp.float32
BF16 = jnp.bfloat16

D_MODEL = 1024
HEAD_DIM = 64
EPS = 1e-6
LANES = 128

SWA_Q_HEADS = 8
SWA_KV_HEADS = 2
SWA_BLOCK = 128
SWA_WIDTH = SWA_Q_HEADS * HEAD_DIM
KV_WIDTH = SWA_KV_HEADS * HEAD_DIM

SSM_WIDTH = D_MODEL // 2
SSM_GROUP_CH = 16
SSM_GROUPS = SSM_WIDTH // SSM_GROUP_CH
SSM_STATE = 64
SSM_STATES = SSM_GROUPS * SSM_STATE
SSM_CHUNKS = SSM_WIDTH // LANES
SSM_CHUNK_STATES = SSM_STATES // SSM_CHUNKS
EVEN_IN = SWA_WIDTH + 2 * KV_WIDTH + SSM_WIDTH

SB_HEADS = D_MODEL // HEAD_DIM
SB_WIDTH = SB_HEADS * HEAD_DIM
LOG2E = math.log2(math.e)
SB_DEAD_LOG2 = 160.0
SB_EXP2_CLAMP = 126.0
SB_PAIRS_PER_STEP = 4
SB_STREAM_GROUP = 2

VMEM_LIMIT_BYTES = 56 * 1024 * 1024
ROW_SPLIT = 2


def _params(*sem):
    return pltpu.CompilerParams(dimension_semantics=sem, vmem_limit_bytes=VMEM_LIMIT_BYTES)


def _const_spec(shape):
    nd = len(shape)
    return pl.BlockSpec(shape, lambda *_: (0,) * nd, pipeline_mode=pl.Buffered(1))


def _row_subblocks(tm):
    step = tm // ROW_SPLIT
    return [slice(i * step, (i + 1) * step) for i in range(ROW_SPLIT)]


def _rmsnorm(x, g):
    ms = jnp.mean(x * x, axis=-1, keepdims=True)
    return x * lax.rsqrt(ms + EPS) * g


def _pair_rmsnorm(x, g2):
    lo = lax.broadcasted_iota(jnp.int32, x.shape, 1) < HEAD_DIM
    sq = x * x
    s_lo = jnp.sum(jnp.where(lo, sq, 0.0), axis=-1, keepdims=True)
    s_hi = jnp.sum(jnp.where(lo, 0.0, sq), axis=-1, keepdims=True)
    ms = jnp.where(lo, s_lo, s_hi) * (1.0 / HEAD_DIM)
    return x * lax.rsqrt(ms + EPS) * g2


def _even_in_kernel(x_ref, g_ref, w_ref, qg_ref, kg_ref, q_ref, k_ref, v_ref, u_ref):
    subs = _row_subblocks(x_ref.shape[1])
    hns = [_rmsnorm(x_ref[0, rows, :], g_ref[...]).astype(BF16) for rows in subs]
    projs = [jnp.dot(hn, w_ref[...], preferred_element_type=F32) for hn in hns]
    scale = HEAD_DIM ** -0.5 * LOG2E
    for proj, rows in zip(projs, subs):
        for p in range(SWA_WIDTH // LANES):
            qp = _pair_rmsnorm(proj[:, p * LANES:(p + 1) * LANES], qg_ref[...])
            q_ref[0, rows, p * LANES:(p + 1) * LANES] = (qp * scale).astype(BF16)
        k = _pair_rmsnorm(proj[:, SWA_WIDTH:SWA_WIDTH + KV_WIDTH], kg_ref[...])
        k_ref[0, rows, :] = k.astype(BF16)
        v_ref[0, rows, :] = proj[:, SWA_WIDTH + KV_WIDTH:SWA_WIDTH + 2 * KV_WIDTH].astype(BF16)
        u_ref[0, rows, :] = proj[:, SWA_WIDTH + 2 * KV_WIDTH:]


def _even_in_proj(x, g, w, qg, kg, tm):
    b, l, d = x.shape
    return pl.pallas_call(
        _even_in_kernel,
        grid=(b, l // tm),
        in_specs=[
            pl.BlockSpec((1, tm, d), lambda i, j: (i, j, 0)),
            _const_spec((1, d)),
            _const_spec((d, EVEN_IN)),
            _const_spec((1, LANES)),
            _const_spec((1, LANES)),
        ],
        out_specs=[
            pl.BlockSpec((1, tm, SWA_WIDTH), lambda i, j: (i, j, 0)),
            pl.BlockSpec((1, tm, KV_WIDTH), lambda i, j: (i, j, 0)),
            pl.BlockSpec((1, tm, KV_WIDTH), lambda i, j: (i, j, 0)),
            pl.BlockSpec((1, tm, SSM_WIDTH), lambda i, j: (i, j, 0)),
        ],
        out_shape=[
            jax.ShapeDtypeStruct((b, l, SWA_WIDTH), BF16),
            jax.ShapeDtypeStruct((b, l, KV_WIDTH), BF16),
            jax.ShapeDtypeStruct((b, l, KV_WIDTH), BF16),
            jax.ShapeDtypeStruct((b, l, SSM_WIDTH), F32),
        ],
        compiler_params=_params("parallel", "parallel"),
        name="even_in_proj",
    )(x, g, w, qg, kg)


def _swa_kernel(sink_ref, q_ref, kc_ref, kp_ref, vc_ref, vp_ref, o_ref, *, nsub):
    n = pl.program_id(1)
    blk = SWA_BLOCK
    kall = jnp.concatenate([kp_ref[0], kc_ref[0]], axis=0).astype(F32)
    vall = jnp.concatenate([vp_ref[0], vc_ref[0]], axis=0).astype(F32)
    lo = lax.broadcasted_iota(jnp.int32, kall.shape, 1) < HEAD_DIM

    def halves(t):
        g0_lo = jnp.where(lo, t, 0.0)
        g1_hi = jnp.where(lo, 0.0, t)
        g0_hi = pltpu.roll(g0_lo, HEAD_DIM, 1)
        g1_lo = pltpu.roll(g1_hi, HEAD_DIM, 1)
        return [[g0_lo.astype(BF16), g0_hi.astype(BF16)], [g1_lo.astype(BF16), g1_hi.astype(BF16)]]

    ks, vs = halves(kall), halves(vall)
    qq = lax.broadcasted_iota(jnp.int32, (blk, 2 * blk), 0)
    kk = lax.broadcasted_iota(jnp.int32, (blk, 2 * blk), 1)
    diff = qq + blk - kk
    band = (diff >= 0) & (diff < blk)
    first_key = jnp.where(n > 0, 0, blk)
    band_first = band & (kk >= first_key)

    streams = []
    for s in range(nsub):
        keys = slice(s * blk, (s + 2) * blk)
        for p in range(SWA_WIDTH // LANES):
            g = (2 * p) // (SWA_Q_HEADS // SWA_KV_HEADS)
            for e in range(2):
                streams.append(dict(
                    q=q_ref[0, s * blk:(s + 1) * blk, p * LANES:(p + 1) * LANES],
                    k=ks[g][e][keys], v=vs[g][e][keys],
                    sink=sink_ref[2 * p + e] * LOG2E, mask=band_first if s == 0 else band))
    group = len(streams) // nsub
    for g0 in range(0, len(streams), group):
        for st in streams[g0:g0 + group]:
            st["z"] = lax.dot_general(st["q"], st["k"], (((1,), (1,)), ((), ())),
                                      preferred_element_type=F32)
        for st in streams[g0:g0 + group]:
            z = jnp.where(st["mask"], st.pop("z"), -jnp.inf)
            m = jnp.maximum(jnp.max(z, axis=-1, keepdims=True), st["sink"])
            pe = jnp.exp2(z - m)
            st["den"] = jnp.sum(pe, axis=-1, keepdims=True) + jnp.exp2(st["sink"] - m)
            st["pe"] = pe.astype(BF16)
        for st in streams[g0:g0 + group]:
            st["o"] = jnp.dot(st.pop("pe"), st["v"], preferred_element_type=F32) / st["den"]
    for s in range(nsub):
        for p in range(SWA_WIDTH // LANES):
            i = (s * (SWA_WIDTH // LANES) + p) * 2
            o_ref[0, s * blk:(s + 1) * blk, p * LANES:(p + 1) * LANES] = (
                streams[i]["o"] + streams[i + 1]["o"]).astype(BF16)


def _swa_attention(sinks, q, k, v, nsub):
    b, l, _ = q.shape
    blk = SWA_BLOCK * nsub
    cur = lambda i, j: (i, j, 0)
    prev = lambda i, j: (i, jnp.maximum(j * nsub - 1, 0), 0)
    return pl.pallas_call(
        functools.partial(_swa_kernel, nsub=nsub),
        grid=(b, l // blk),
        in_specs=[
            pl.BlockSpec(memory_space=pltpu.SMEM),
            pl.BlockSpec((1, blk, SWA_WIDTH), cur),
            pl.BlockSpec((1, blk, KV_WIDTH), cur),
            pl.BlockSpec((1, SWA_BLOCK, KV_WIDTH), prev),
            pl.BlockSpec((1, blk, KV_WIDTH), cur),
            pl.BlockSpec((1, SWA_BLOCK, KV_WIDTH), prev),
        ],
        out_specs=pl.BlockSpec((1, blk, SWA_WIDTH), cur),
        out_shape=jax.ShapeDtypeStruct((b, l, SWA_WIDTH), BF16),
        compiler_params=_params("parallel", "parallel"),
        name="swa_attention",
    )(sinks, q, k, k, v, v)


def _ssm_disc_kernel(are_ref, aim_ref, ldt_ref, lre_ref, lim_ref, wre_ref, wim_ref):
    a_re, a_im = are_ref[...], aim_ref[...]
    dt = jnp.exp(ldt_ref[...])
    mag = jnp.exp(a_re * dt)
    lam_re = mag * jnp.cos(a_im * dt)
    lam_im = mag * jnp.sin(a_im * dt)
    den = a_re * a_re + a_im * a_im
    lre_ref[...] = lam_re
    lim_ref[...] = lam_im
    wre_ref[...] = ((lam_re - 1.0) * a_re + lam_im * a_im) / den
    wim_ref[...] = (lam_im * a_re - (lam_re - 1.0) * a_im) / den


def _ssm_bbar_kernel(wre_ref, wim_ref, bre_ref, bim_ref, ore_ref, oim_ref):
    w_re, w_im, b_re, b_im = wre_ref[...], wim_ref[...], bre_ref[...], bim_ref[...]
    ore_ref[...] = w_re * b_re - w_im * b_im
    oim_ref[...] = w_re * b_im + w_im * b_re


def _block_diag(t):
    ch, gl, r, c = t.shape
    eye = jnp.eye(gl, dtype=t.dtype)
    return (t[:, :, :, None, :] * eye[None, :, None, :, None]).reshape(ch, gl * r, gl * c)


def _ssm_prepare(a_re, a_im, b_re, b_im, c_re, c_im, log_dt, batch):
    g, n, p = SSM_GROUPS, SSM_STATE, SSM_GROUP_CH
    gn = jax.ShapeDtypeStruct((g, n), F32)
    ldt = jnp.broadcast_to(log_dt[:, None], (g, n))
    lam_re, lam_im, w_re, w_im = pl.pallas_call(
        _ssm_disc_kernel, out_shape=[gn, gn, gn, gn], name="ssm_discretise")(a_re, a_im, ldt)
    gnp = jax.ShapeDtypeStruct((g, n * p), F32)
    bb_re, bb_im = pl.pallas_call(_ssm_bbar_kernel, out_shape=[gnp, gnp], name="ssm_bbar")(
        jnp.repeat(w_re, p, axis=1), jnp.repeat(w_im, p, axis=1),
        b_re.reshape(g, n * p), b_im.reshape(g, n * p))
    gl = g // SSM_CHUNKS
    to_in = lambda t: _block_diag(
        t.reshape(SSM_CHUNKS, gl, n, p).transpose(0, 1, 3, 2)).astype(BF16)
    to_out = lambda t: _block_diag(
        t.reshape(SSM_CHUNKS, gl, p, n).transpose(0, 1, 3, 2)).astype(BF16)
    bcast = lambda t: jnp.broadcast_to(t.reshape(1, SSM_STATES), (batch, SSM_STATES))
    return (to_in(bb_re), to_in(bb_im), bcast(lam_re), bcast(lam_im), to_out(c_re), to_out(c_im))


def _ssm_kernel(u_ref, bre_ref, bim_ref, lre_ref, lim_ref, cre_ref, cim_ref, d_ref, wg_ref, bg_ref,
                o_ref, ubuf_ref, h0_ref, h1_ref, h2_ref, h3_ref, sre_ref, sim_ref,
                *, batch, steps, pitch):
    @pl.when(pl.program_id(0) == 0)
    def _():
        sre_ref[...] = jnp.zeros_like(sre_ref)
        sim_ref[...] = jnp.zeros_like(sim_ref)
        ubuf_ref[...] = jnp.zeros_like(ubuf_ref)

    for b in range(batch):
        ubuf_ref[b * pitch:b * pitch + steps, :] = u_ref[b]
    u = ubuf_ref[...]
    ub = u.astype(BF16)
    tiles = SSM_CHUNK_STATES // LANES
    ys = [None] * SSM_CHUNKS
    h_refs = (h0_ref, h1_ref, h2_ref, h3_ref)

    def input_matmul(c, part):
        w_ref = (bre_ref, bim_ref)[part]
        bu = jnp.dot(ub[:, c * LANES:(c + 1) * LANES], w_ref[c], preferred_element_type=F32)
        for i in range(tiles):
            h_refs[c][part, i] = bu[:, i * LANES:(i + 1) * LANES]

    def output_matmul(c):
        hs = [jnp.concatenate([h_refs[c][part, i] for i in range(tiles)], axis=1).astype(BF16)
              for part in range(2)]
        ys[c] = (jnp.dot(hs[0], cre_ref[c], preferred_element_type=F32)
                 - jnp.dot(hs[1], cim_ref[c], preferred_element_type=F32))

    def scan(chunks, fillers):
        slabs = [(c, i) for c in chunks for i in range(tiles)]
        lanes = [slice((c * tiles + i) * LANES, (c * tiles + i + 1) * LANES) for c, i in slabs]
        h_re = [sre_ref[:, ln] for ln in lanes]
        h_im = [sim_ref[:, ln] for ln in lanes]
        l_re = [lre_ref[:, ln] for ln in lanes]
        l_im = [lim_ref[:, ln] for ln in lanes]
        fillers = list(fillers)
        every = max(1, steps // max(1, len(fillers)))
        for t in range(steps):
            rows = pl.ds(t, batch, stride=pitch)
            for i, (c, tile) in enumerate(slabs):
                n_re = l_re[i] * h_re[i] - l_im[i] * h_im[i] + h_refs[c][0, tile, rows, :]
                n_im = l_re[i] * h_im[i] + l_im[i] * h_re[i] + h_refs[c][1, tile, rows, :]
                h_refs[c][0, tile, rows, :] = n_re
                h_refs[c][1, tile, rows, :] = n_im
                h_re[i], h_im[i] = n_re, n_im
            if fillers and (t + 1) % every == 0:
                fillers.pop(0)()
        for f in fillers:
            f()
        for i, ln in enumerate(lanes):
            sre_ref[:, ln] = h_re[i]
            sim_ref[:, ln] = h_im[i]

    first, second = (0, 1), (2, 3)
    for c in first:
        input_matmul(c, 0)
        input_matmul(c, 1)
    scan(first, [functools.partial(input_matmul, c, part) for c in second for part in range(2)])
    scan(second, [functools.partial(output_matmul, c) for c in first])
    for c in second:
        output_matmul(c)
    y = jnp.concatenate(ys, axis=-1) + d_ref[...] * u
    y = jax.nn.gelu(y)
    gate = jnp.dot(y.astype(BF16), wg_ref[...], preferred_element_type=F32) + bg_ref[...]
    out = y * jax.nn.sigmoid(gate)
    for b in range(batch):
        o_ref[b] = out[b * pitch:b * pitch + steps].astype(BF16)


def _ssm_mixer(u, mats, d_skip, w_glu, b_glu, steps):
    batch, l, width = u.shape
    assert steps % 8 == 0
    pitch = steps + 4
    b_in_re, b_in_im, lam_re, lam_im, c_out_re, c_out_im = mats
    block = pl.BlockSpec((batch, steps, width), lambda i: (0, i, 0))
    return pl.pallas_call(
        functools.partial(_ssm_kernel, batch=batch, steps=steps, pitch=pitch),
        grid=(l // steps,),
        in_specs=[
            block,
            _const_spec(b_in_re.shape), _const_spec(b_in_im.shape),
            _const_spec(lam_re.shape), _const_spec(lam_im.shape),
            _const_spec(c_out_re.shape), _const_spec(c_out_im.shape),
            _const_spec((1, width)), _const_spec((width, width)), _const_spec((1, width)),
        ],
        out_specs=block,
        out_shape=jax.ShapeDtypeStruct((batch, l, width), BF16),
        scratch_shapes=[
            pltpu.VMEM((batch * pitch, width), F32),
            *[pltpu.VMEM((2, SSM_CHUNK_STATES // LANES, batch * pitch, LANES), F32)
              for _ in range(SSM_CHUNKS)],
            pltpu.VMEM((batch, SSM_STATES), F32), pltpu.VMEM((batch, SSM_STATES), F32),
        ],
        compiler_params=_params("arbitrary"),
        name="ssm_mixer",
    )(u, b_in_re, b_in_im, lam_re, lam_im, c_out_re, c_out_im, d_skip, w_glu, b_glu)


def _odd_in_kernel(x_ref, g_ref, w_ref, q_ref, k_ref, v_ref):
    subs = _row_subblocks(x_ref.shape[1])
    hns = [_rmsnorm(x_ref[0, rows, :], g_ref[...]).astype(BF16) for rows in subs]
    projs = [jnp.dot(hn, w_ref[...], preferred_element_type=F32) for hn in hns]
    for proj, rows in zip(projs, subs):
        q_ref[0, rows, :] = (proj[:, :SB_WIDTH] * (HEAD_DIM ** -0.5 * LOG2E)).astype(BF16)
        k_ref[0, rows, :] = proj[:, SB_WIDTH:2 * SB_WIDTH].astype(BF16)
        v_ref[0, rows, :] = proj[:, 2 * SB_WIDTH:].astype(BF16)


def _odd_in_proj(x, g, w, tm):
    b, l, d = x.shape
    row = pl.BlockSpec((1, tm, SB_WIDTH), lambda i, j: (i, j, 0))
    out = jax.ShapeDtypeStruct((b, l, SB_WIDTH), BF16)
    return pl.pallas_call(
        _odd_in_kernel,
        grid=(b, l // tm),
        in_specs=[pl.BlockSpec((1, tm, d), lambda i, j: (i, j, 0)),
                  _const_spec((1, d)), _const_spec((d, 3 * SB_WIDTH))],
        out_specs=[row, row, row],
        out_shape=[out, out, out],
        compiler_params=_params("parallel", "parallel"),
        name="odd_in_proj",
    )(x, g, w)


def _sb_streams(streams, tri):
    m = streams[0][0].shape[0]
    half = m // 2
    strict = (lax.broadcasted_iota(jnp.int32, (half, half), 1)
              < lax.broadcasted_iota(jnp.int32, (half, half), 0))

    def lower_triangle(fn, *blocks):
        quad = lambda r, c: fn(*[blk[r * half:(r + 1) * half, c * half:(c + 1) * half]
                                 for blk in blocks])
        top_left = jnp.where(strict, quad(0, 0), 0.0)
        bottom_right = jnp.where(strict, quad(1, 1), 0.0)
        top = jnp.concatenate([top_left, jnp.zeros_like(top_left)], axis=1)
        return jnp.concatenate([top, jnp.concatenate([quad(1, 0), bottom_right], axis=1)], axis=0)

    def softplus2(z):
        return jnp.maximum(z, jnp.log(1.0 + jnp.exp2(jnp.minimum(z, SB_EXP2_CLAMP))) * LOG2E)

    def scores(s):
        s["z"] = lax.dot_general(s["q"], s["k"], (((1,), (1,)), ((), ())),
                                 preferred_element_type=F32)

    def softplus(s):
        sp = lower_triangle(softplus2, s["z"]) if s["diag"] else softplus2(s["z"])
        s["sp"] = sp.astype(BF16)

    def suffix_sums(s):
        s["w"] = jnp.dot(s.pop("sp"), tri, preferred_element_type=F32)

    def weights(s):
        weight = lambda z, w: jnp.exp2(z - w)
        z = s.pop("z")
        p = lower_triangle(weight, z, s["w"]) if s["diag"] else weight(z, s["w"])
        s["p"] = p.astype(BF16)

    def values(s):
        s["pv"] = jnp.dot(s.pop("p"), s["v"], preferred_element_type=F32)

    stages = (scores, softplus, suffix_sums, weights, values)
    state = [dict(q=q, k=kh, v=vb, diag=diag) for q, kh, vb, diag in streams]
    groups = [state[i:i + SB_STREAM_GROUP] for i in range(0, len(state), SB_STREAM_GROUP)]
    for tick in range(len(groups) + len(stages) - 1):
        for g, members in enumerate(groups):
            if 0 <= tick - g < len(stages):
                for s in members:
                    stages[tick - g](s)
    return [(s["pv"], jnp.broadcast_to(s["w"][:, 0:1], (s["pv"].shape[0], LANES))) for s in state]


def _sb_kernel(q_ref, k_ref, v_ref, tri_ref, o_ref, c_ref, acc_ref, *, th, npair):
    qi = pl.program_id(2)
    lo_k = lax.broadcasted_iota(jnp.int32, (th, LANES), 1) < HEAD_DIM
    lo_q = lax.broadcasted_iota(jnp.int32, (th, LANES), 1) < HEAD_DIM
    tri = tri_ref[...]

    def load_kv(pair, j):
        rows = pl.ds(pl.multiple_of(j * th, th), th)
        lanes = slice(pair * LANES, (pair + 1) * LANES)
        kb = k_ref[0, rows, lanes]
        return (jnp.where(lo_k, kb, 0), jnp.where(lo_k, 0, kb)), v_ref[0, rows, lanes]

    def load_q(pair, half):
        return q_ref[0, half * th:(half + 1) * th, pair * LANES:(pair + 1) * LANES]

    j_diag = (2 * qi, 2 * qi + 1)
    has_left = (qi > 0).astype(F32)

    streams = []
    for pair in range(npair):
        kv_left = load_kv(pair, jnp.maximum(2 * qi - 1, 0))
        kv_diag = (load_kv(pair, j_diag[0]), load_kv(pair, j_diag[1]))
        for half in range(2):
            q = load_q(pair, half)
            (kd, vd) = kv_diag[half]
            (kl, vl) = kv_diag[0] if half == 1 else kv_left
            for h in range(2):
                streams += [(q, kd[h], vd, True), (q, kl[h], vl, False)]
    outs = iter(_sb_streams(streams, tri))
    for pair in range(npair):
        for half in range(2):
            for h in range(2):
                (pv_d, c_d), (pv_l, c_l) = next(outs), next(outs)
                scale = jnp.exp2(-c_d)
                if half == 0:
                    scale, c_l = scale * has_left, c_l * has_left
                acc_ref[pair, half, h] = pv_d + pv_l * scale
                c_ref[pair, half, h] = c_d + c_l

    def visit_rest(pair, half):
        q = load_q(pair, half)
        n_rest = jnp.maximum(j_diag[half] - 1, 0)

        def cond(carry):
            i, c_min = carry
            return jnp.logical_and(i < n_rest, c_min < SB_DEAD_LOG2)

        def body(carry):
            i, _ = carry
            ks, vb = load_kv(pair, n_rest - 1 - i)
            outs = _sb_streams([(q, ks[0], vb, False), (q, ks[1], vb, False)], tri)
            for h, (pv, c_blk) in enumerate(outs):
                c = c_ref[pair, half, h]
                acc_ref[pair, half, h] += pv * jnp.exp2(-c)
                c_ref[pair, half, h] = c + c_blk
            return i + 1, jnp.min(c_ref[pair, half])

        lax.while_loop(cond, body, (jnp.int32(0), jnp.min(c_ref[pair, half])))

    @pl.when(jnp.min(c_ref[...]) < SB_DEAD_LOG2)
    def _():
        for pair in range(npair):
            for half in range(2):
                visit_rest(pair, half)

    for pair in range(npair):
        for half in range(2):
            o_ref[0, half * th:(half + 1) * th, pair * LANES:(pair + 1) * LANES] = jnp.where(
                lo_q, acc_ref[pair, half, 0], acc_ref[pair, half, 1]).astype(BF16)


def _sb_attention(q, k, v, th, npair):
    b, l, width = q.shape
    tri = (jnp.arange(th)[:, None] >= jnp.arange(th)[None, :]).astype(BF16)
    tq = 2 * th
    lanes = npair * LANES
    state = pltpu.VMEM((npair, 2, 2, th, LANES), F32)
    return pl.pallas_call(
        functools.partial(_sb_kernel, th=th, npair=npair),
        grid=(b, width // lanes, l // tq),
        in_specs=[
            pl.BlockSpec((1, tq, lanes), lambda i, h, j: (i, j, h)),
            pl.BlockSpec((1, l, lanes), lambda i, h, j: (i, 0, h)),
            pl.BlockSpec((1, l, lanes), lambda i, h, j: (i, 0, h)),
            _const_spec((th, th)),
        ],
        out_specs=pl.BlockSpec((1, tq, lanes), lambda i, h, j: (i, j, h)),
        out_shape=jax.ShapeDtypeStruct((b, l, width), BF16),
        scratch_shapes=[state, state],
        compiler_params=_params("parallel", "parallel", "arbitrary"),
        name="sb_attention",
    )(q, k, v, tri)


def _post_kernel(*refs, n_mix):
    x_ref = refs[0]
    mix_refs = refs[1:1 + n_mix]
    wo_refs = refs[1 + n_mix:1 + 2 * n_mix]
    g_ref, wg_ref, wu_ref, wd_ref, o_ref = refs[1 + 2 * n_mix:]
    subs = _row_subblocks(x_ref.shape[1])
    xs = [x_ref[0, rows, :] for rows in subs]
    for m_ref, w_ref in zip(mix_refs, wo_refs):
        xs = [x + jnp.dot(m_ref[0, rows, :], w_ref[...], preferred_element_type=F32)
              for x, rows in zip(xs, subs)]
    hns = [_rmsnorm(x, g_ref[...]).astype(BF16) for x in xs]
    gates = [jnp.dot(hn, wg_ref[...], preferred_element_type=F32) for hn in hns]
    ups = [jnp.dot(hn, wu_ref[...], preferred_element_type=F32) for hn in hns]
    acts = [(gate * jax.nn.sigmoid(gate) * up).astype(BF16) for gate, up in zip(gates, ups)]
    for x, act, rows in zip(xs, acts, subs):
        o_ref[0, rows, :] = x + jnp.dot(act, wd_ref[...], preferred_element_type=F32)


def _post(x, mixes, mix_specs, w_outs, g, w_gate, w_up, w_down, tm):
    b, l, d = x.shape
    xspec = pl.BlockSpec((1, tm, d), lambda i, j: (i, j, 0))
    return pl.pallas_call(
        functools.partial(_post_kernel, n_mix=len(mixes)),
        grid=(b, l // tm),
        in_specs=[xspec, *mix_specs, *[_const_spec(w.shape) for w in w_outs],
                  _const_spec((1, d)), _const_spec(w_gate.shape), _const_spec(w_up.shape),
                  _const_spec(w_down.shape)],
        out_specs=xspec,
        out_shape=jax.ShapeDtypeStruct(x.shape, F32),
        compiler_params=_params("parallel", "parallel"),
        name="out_proj_ffn",
    )(x, *mixes, *w_outs, g, w_gate, w_up, w_down)


def _tiles(l):
    tm = min(512, l)
    th = min(256, l // 2)
    steps = min(64, l)
    nsub = min(4, l // SWA_BLOCK)
    return tm, th, steps, nsub


def kernel(x, even_norm, even_w_in, q_norm, k_norm, sinks, ssm_a_re, ssm_a_im, ssm_b_re, ssm_b_im,
           ssm_c_re, ssm_c_im, ssm_d, ssm_log_dt, ssm_w_glu, ssm_b_glu, even_w_out, odd_norm,
           odd_w_in, odd_w_out, ffn_norm, ffn_w_gate, ffn_w_up, ffn_w_down):
    b, l, d = x.shape
    assert d == D_MODEL and l % SWA_BLOCK == 0
    tm, th, steps, nsub = _tiles(l)
    depth = ffn_norm.shape[0]
    bf = lambda t: t.astype(BF16)
    row = lambda t: t.reshape(1, -1)
    for layer in range(depth):
        i = layer // 2
        if layer % 2 == 0:
            q, k, v, u = _even_in_proj(
                x, row(even_norm[i]), bf(even_w_in[i]),
                row(jnp.tile(q_norm[i], 2)), row(jnp.tile(k_norm[i], 2)), tm)
            o_attn = _swa_attention(sinks[i], q, k, v, nsub)
            mats = _ssm_prepare(ssm_a_re[i], ssm_a_im[i], ssm_b_re[i], ssm_b_im[i],
                                ssm_c_re[i], ssm_c_im[i], ssm_log_dt[i], b)
            o_ssm = _ssm_mixer(u, mats, row(ssm_d[i]), bf(ssm_w_glu[i]), row(ssm_b_glu[i]), steps)
            mixes = [o_attn, o_ssm]
            mix_specs = [pl.BlockSpec((1, tm, SWA_WIDTH), lambda bi, j: (bi, j, 0)),
                         pl.BlockSpec((1, tm, SSM_WIDTH), lambda bi, j: (bi, j, 0))]
            w_outs = [bf(even_w_out[i][:SWA_WIDTH]), bf(even_w_out[i][SWA_WIDTH:])]
        else:
            q, k, v = _odd_in_proj(x, row(odd_norm[i]), bf(odd_w_in[i]), tm)
            mixes = [_sb_attention(q, k, v, th, SB_PAIRS_PER_STEP)]
            mix_specs = [pl.BlockSpec((1, tm, SB_WIDTH), lambda bi, j: (bi, j, 0))]
            w_outs = [bf(odd_w_out[i])]
        x = _post(x, mixes, mix_specs, w_outs, row(ffn_norm[layer]), bf(ffn_w_gate[layer]),
                  bf(ffn_w_up[layer]), bf(ffn_w_down[layer]), tm)
    return x
```

```python
import functools
import math

import jax
import jax.numpy as jnp
from jax import lax
from jax.experimental import pallas as pl
from jax.experimental.pallas import tpu as pltpu

F32 = jnp.float32
BF16 = jnp.bfloat16

D_MODEL = 1024
HEAD_DIM = 64
EPS = 1e-6
LANES = 128

SWA_Q_HEADS = 8
SWA_KV_HEADS = 2
SWA_BLOCK = 128
SWA_WIDTH = SWA_Q_HEADS * HEAD_DIM
KV_WIDTH = SWA_KV_HEADS * HEAD_DIM

SSM_WIDTH = D_MODEL // 2
SSM_GROUP_CH = 16
SSM_GROUPS = SSM_WIDTH // SSM_GROUP_CH
SSM_STATE = 64
SSM_STATES = SSM_GROUPS * SSM_STATE
SSM_CHUNKS = SSM_WIDTH // LANES
SSM_CHUNK_STATES = SSM_STATES // SSM_CHUNKS
EVEN_IN = SWA_WIDTH + 2 * KV_WIDTH + SSM_WIDTH

SB_HEADS = D_MODEL // HEAD_DIM
SB_WIDTH = SB_HEADS * HEAD_DIM
LOG2E = math.log2(math.e)
SB_DEAD_LOG2 = 160.0
SB_EXP2_CLAMP = 126.0
SB_PAIRS_PER_STEP = 4
SB_STREAM_GROUP = 2

VMEM_LIMIT_BYTES = 56 * 1024 * 1024
ROW_SPLIT = 2


def _params(*sem):
    return pltpu.CompilerParams(dimension_semantics=sem, vmem_limit_bytes=VMEM_LIMIT_BYTES)


def _const_spec(shape):
    nd = len(shape)
    return pl.BlockSpec(shape, lambda *_: (0,) * nd, pipeline_mode=pl.Buffered(1))


def _row_subblocks(tm):
    step = tm // ROW_SPLIT
    return [slice(i * step, (i + 1) * step) for i in range(ROW_SPLIT)]


def _rmsnorm(x, g):
    ms = jnp.mean(x * x, axis=-1, keepdims=True)
    return x * lax.rsqrt(ms + EPS) * g


def _pair_rmsnorm(x, g2):
    lo = lax.broadcasted_iota(jnp.int32, x.shape, 1) < HEAD_DIM
    sq = x * x
    s_lo = jnp.sum(jnp.where(lo, sq, 0.0), axis=-1, keepdims=True)
    s_hi = jnp.sum(jnp.where(lo, 0.0, sq), axis=-1, keepdims=True)
    ms = jnp.where(lo, s_lo, s_hi) * (1.0 / HEAD_DIM)
    return x * lax.rsqrt(ms + EPS) * g2


def _even_in_kernel(x_ref, g_ref, w_ref, qg_ref, kg_ref, q_ref, k_ref, v_ref, u_ref):
    subs = _row_subblocks(x_ref.shape[1])
    hns = [_rmsnorm(x_ref[0, rows, :], g_ref[...]).astype(BF16) for rows in subs]
    projs = [jnp.dot(hn, w_ref[...], preferred_element_type=F32) for hn in hns]
    scale = HEAD_DIM ** -0.5 * LOG2E
    for proj, rows in zip(projs, subs):
        for p in range(SWA_WIDTH // LANES):
            qp = _pair_rmsnorm(proj[:, p * LANES:(p + 1) * LANES], qg_ref[...])
            q_ref[0, rows, p * LANES:(p + 1) * LANES] = (qp * scale).astype(BF16)
        k = _pair_rmsnorm(proj[:, SWA_WIDTH:SWA_WIDTH + KV_WIDTH], kg_ref[...])
        k_ref[0, rows, :] = k.astype(BF16)
        v_ref[0, rows, :] = proj[:, SWA_WIDTH + KV_WIDTH:SWA_WIDTH + 2 * KV_WIDTH].astype(BF16)
        u_ref[0, rows, :] = proj[:, SWA_WIDTH + 2 * KV_WIDTH:]


def _even_in_proj(x, g, w, qg, kg, tm):
    b, l, d = x.shape
    return pl.pallas_call(
        _even_in_kernel,
        grid=(b, l // tm),
        in_specs=[
            pl.BlockSpec((1, tm, d), lambda i, j: (i, j, 0)),
            _const_spec((1, d)),
            _const_spec((d, EVEN_IN)),
            _const_spec((1, LANES)),
            _const_spec((1, LANES)),
        ],
        out_specs=[
            pl.BlockSpec((1, tm, SWA_WIDTH), lambda i, j: (i, j, 0)),
            pl.BlockSpec((1, tm, KV_WIDTH), lambda i, j: (i, j, 0)),
            pl.BlockSpec((1, tm, KV_WIDTH), lambda i, j: (i, j, 0)),
            pl.BlockSpec((1, tm, SSM_WIDTH), lambda i, j: (i, j, 0)),
        ],
        out_shape=[
            jax.ShapeDtypeStruct((b, l, SWA_WIDTH), BF16),
            jax.ShapeDtypeStruct((b, l, KV_WIDTH), BF16),
            jax.ShapeDtypeStruct((b, l, KV_WIDTH), BF16),
            jax.ShapeDtypeStruct((b, l, SSM_WIDTH), F32),
        ],
        compiler_params=_params("parallel", "parallel"),
        name="even_in_proj",
    )(x, g, w, qg, kg)


def _swa_kernel(sink_ref, q_ref, kc_ref, kp_ref, vc_ref, vp_ref, o_ref, *, nsub):
    n = pl.program_id(1)
    blk = SWA_BLOCK
    kall = jnp.concatenate([kp_ref[0], kc_ref[0]], axis=0).astype(F32)
    vall = jnp.concatenate([vp_ref[0], vc_ref[0]], axis=0).astype(F32)
    lo = lax.broadcasted_iota(jnp.int32, kall.shape, 1) < HEAD_DIM

    def halves(t):
        g0_lo = jnp.where(lo, t, 0.0)
        g1_hi = jnp.where(lo, 0.0, t)
        g0_hi = pltpu.roll(g0_lo, HEAD_DIM, 1)
        g1_lo = pltpu.roll(g1_hi, HEAD_DIM, 1)
        return [[g0_lo.astype(BF16), g0_hi.astype(BF16)], [g1_lo.astype(BF16), g1_hi.astype(BF16)]]

    ks, vs = halves(kall), halves(vall)
    qq = lax.broadcasted_iota(jnp.int32, (blk, 2 * blk), 0)
    kk = lax.broadcasted_iota(jnp.int32, (blk, 2 * blk), 1)
    diff = qq + blk - kk
    band = (diff >= 0) & (diff < blk)
    first_key = jnp.where(n > 0, 0, blk)
    band_first = band & (kk >= first_key)

    streams = []
    for s in range(nsub):
        keys = slice(s * blk, (s + 2) * blk)
        for p in range(SWA_WIDTH // LANES):
            g = (2 * p) // (SWA_Q_HEADS // SWA_KV_HEADS)
            for e in range(2):
                streams.append(dict(
                    q=q_ref[0, s * blk:(s + 1) * blk, p * LANES:(p + 1) * LANES],
                    k=ks[g][e][keys], v=vs[g][e][keys],
                    sink=sink_ref[2 * p + e] * LOG2E, mask=band_first if s == 0 else band))
    group = len(streams) // nsub
    for g0 in range(0, len(streams), group):
        for st in streams[g0:g0 + group]:
            st["z"] = lax.dot_general(st["q"], st["k"], (((1,), (1,)), ((), ())),
                                      preferred_element_type=F32)
        for st in streams[g0:g0 + group]:
            z = jnp.where(st["mask"], st.pop("z"), -jnp.inf)
            m = jnp.maximum(jnp.max(z, axis=-1, keepdims=True), st["sink"])
            pe = jnp.exp2(z - m)
            st["den"] = jnp.sum(pe, axis=-1, keepdims=True) + jnp.exp2(st["sink"] - m)
            st["pe"] = pe.astype(BF16)
        for st in streams[g0:g0 + group]:
            st["o"] = jnp.dot(st.pop("pe"), st["v"], preferred_element_type=F32) / st["den"]
    for s in range(nsub):
        for p in range(SWA_WIDTH // LANES):
            i = (s * (SWA_WIDTH // LANES) + p) * 2
            o_ref[0, s * blk:(s + 1) * blk, p * LANES:(p + 1) * LANES] = (
                streams[i]["o"] + streams[i + 1]["o"]).astype(BF16)


def _swa_attention(sinks, q, k, v, nsub):
    b, l, _ = q.shape
    blk = SWA_BLOCK * nsub
    cur = lambda i, j: (i, j, 0)
    prev = lambda i, j: (i, jnp.maximum(j * nsub - 1, 0), 0)
    return pl.pallas_call(
        functools.partial(_swa_kernel, nsub=nsub),
        grid=(b, l // blk),
        in_specs=[
            pl.BlockSpec(memory_space=pltpu.SMEM),
            pl.BlockSpec((1, blk, SWA_WIDTH), cur),
            pl.BlockSpec((1, blk, KV_WIDTH), cur),
            pl.BlockSpec((1, SWA_BLOCK, KV_WIDTH), prev),
            pl.BlockSpec((1, blk, KV_WIDTH), cur),
            pl.BlockSpec((1, SWA_BLOCK, KV_WIDTH), prev),
        ],
        out_specs=pl.BlockSpec((1, blk, SWA_WIDTH), cur),
        out_shape=jax.ShapeDtypeStruct((b, l, SWA_WIDTH), BF16),
        compiler_params=_params("parallel", "parallel"),
        name="swa_attention",
    )(sinks, q, k, k, v, v)


def _ssm_disc_kernel(are_ref, aim_ref, ldt_ref, lre_ref, lim_ref, wre_ref, wim_ref):
    a_re, a_im = are_ref[...], aim_ref[...]
    dt = jnp.exp(ldt_ref[...])
    mag = jnp.exp(a_re * dt)
    lam_re = mag * jnp.cos(a_im * dt)
    lam_im = mag * jnp.sin(a_im * dt)
    den = a_re * a_re + a_im * a_im
    lre_ref[...] = lam_re
    lim_ref[...] = lam_im
    wre_ref[...] = ((lam_re - 1.0) * a_re + lam_im * a_im) / den
    wim_ref[...] = (lam_im * a_re - (lam_re - 1.0) * a_im) / den


def _ssm_bbar_kernel(wre_ref, wim_ref, bre_ref, bim_ref, ore_ref, oim_ref):
    w_re, w_im, b_re, b_im = wre_ref[...], wim_ref[...], bre_ref[...], bim_ref[...]
    ore_ref[...] = w_re * b_re - w_im * b_im
    oim_ref[...] = w_re * b_im + w_im * b_re


def _block_diag(t):
    ch, gl, r, c = t.shape
    eye = jnp.eye(gl, dtype=t.dtype)
    return (t[:, :, :, None, :] * eye[None, :, None, :, None]).reshape(ch, gl * r, gl * c)


def _ssm_prepare(a_re, a_im, b_re, b_im, c_re, c_im, log_dt, batch):
    g, n, p = SSM_GROUPS, SSM_STATE, SSM_GROUP_CH
    gn = jax.ShapeDtypeStruct((g, n), F32)
    ldt = jnp.broadcast_to(log_dt[:, None], (g, n))
    lam_re, lam_im, w_re, w_im = pl.pallas_call(
        _ssm_disc_kernel, out_shape=[gn, gn, gn, gn], name="ssm_discretise")(a_re, a_im, ldt)
    gnp = jax.ShapeDtypeStruct((g, n * p), F32)
    bb_re, bb_im = pl.pallas_call(_ssm_bbar_kernel, out_shape=[gnp, gnp], name="ssm_bbar")(
        jnp.repeat(w_re, p, axis=1), jnp.repeat(w_im, p, axis=1),
        b_re.reshape(g, n * p), b_im.reshape(g, n * p))
    gl = g // SSM_CHUNKS
    to_in = lambda t: _block_diag(
        t.reshape(SSM_CHUNKS, gl, n, p).transpose(0, 1, 3, 2)).astype(BF16)
    to_out = lambda t: _block_diag(
        t.reshape(SSM_CHUNKS, gl, p, n).transpose(0, 1, 3, 2)).astype(BF16)
    bcast = lambda t: jnp.broadcast_to(t.reshape(1, SSM_STATES), (batch, SSM_STATES))
    return (to_in(bb_re), to_in(bb_im), bcast(lam_re), bcast(lam_im), to_out(c_re), to_out(c_im))


def _ssm_kernel(u_ref, bre_ref, bim_ref, lre_ref, lim_ref, cre_ref, cim_ref, d_ref, wg_ref, bg_ref,
                o_ref, ubuf_ref, h0_ref, h1_ref, h2_ref, h3_ref, sre_ref, sim_ref,
                *, batch, steps, pitch):
    @pl.when(pl.program_id(0) == 0)
    def _():
        sre_ref[...] = jnp.zeros_like(sre_ref)
        sim_ref[...] = jnp.zeros_like(sim_ref)
        ubuf_ref[...] = jnp.zeros_like(ubuf_ref)

    for b in range(batch):
        ubuf_ref[b * pitch:b * pitch + steps, :] = u_ref[b]
    u = ubuf_ref[...]
    ub = u.astype(BF16)
    tiles = SSM_CHUNK_STATES // LANES
    ys = [None] * SSM_CHUNKS
    h_refs = (h0_ref, h1_ref, h2_ref, h3_ref)

    def input_matmul(c, part):
        w_ref = (bre_ref, bim_ref)[part]
        bu = jnp.dot(ub[:, c * LANES:(c + 1) * LANES], w_ref[c], preferred_element_type=F32)
        for i in range(tiles):
            h_refs[c][part, i] = bu[:, i * LANES:(i + 1) * LANES]

    def output_matmul(c):
        hs = [jnp.concatenate([h_refs[c][part, i] for i in range(tiles)], axis=1).astype(BF16)
              for part in range(2)]
        ys[c] = (jnp.dot(hs[0], cre_ref[c], preferred_element_type=F32)
                 - jnp.dot(hs[1], cim_ref[c], preferred_element_type=F32))

    def scan(chunks, fillers):
        slabs = [(c, i) for c in chunks for i in range(tiles)]
        lanes = [slice((c * tiles + i) * LANES, (c * tiles + i + 1) * LANES) for c, i in slabs]
        h_re = [sre_ref[:, ln] for ln in lanes]
        h_im = [sim_ref[:, ln] for ln in lanes]
        l_re = [lre_ref[:, ln] for ln in lanes]
        l_im = [lim_ref[:, ln] for ln in lanes]
        fillers = list(fillers)
        every = max(1, steps // max(1, len(fillers)))
        for t in range(steps):
            rows = pl.ds(t, batch, stride=pitch)
            for i, (c, tile) in enumerate(slabs):
                n_re = l_re[i] * h_re[i] - l_im[i] * h_im[i] + h_refs[c][0, tile, rows, :]
                n_im = l_re[i] * h_im[i] + l_im[i] * h_re[i] + h_refs[c][1, tile, rows, :]
                h_refs[c][0, tile, rows, :] = n_re
                h_refs[c][1, tile, rows, :] = n_im
                h_re[i], h_im[i] = n_re, n_im
            if fillers and (t + 1) % every == 0:
                fillers.pop(0)()
        for f in fillers:
            f()
        for i, ln in enumerate(lanes):
            sre_ref[:, ln] = h_re[i]
            sim_ref[:, ln] = h_im[i]

    first, second = (0, 1), (2, 3)
    for c in first:
        input_matmul(c, 0)
        input_matmul(c, 1)
    scan(first, [functools.partial(input_matmul, c, part) for c in second for part in range(2)])
    scan(second, [functools.partial(output_matmul, c) for c in first])
    for c in second:
        output_matmul(c)
    y = jnp.concatenate(ys, axis=-1) + d_ref[...] * u
    y = jax.nn.gelu(y)
    gate = jnp.dot(y.astype(BF16), wg_ref[...], preferred_element_type=F32) + bg_ref[...]
    out = y * jax.nn.sigmoid(gate)
    for b in range(batch):
        o_ref[b] = out[b * pitch:b * pitch + steps].astype(BF16)


def _ssm_mixer(u, mats, d_skip, w_glu, b_glu, steps):
    batch, l, width = u.shape
    assert steps % 8 == 0
    pitch = steps + 4
    b_in_re, b_in_im, lam_re, lam_im, c_out_re, c_out_im = mats
    block = pl.BlockSpec((batch, steps, width), lambda i: (0, i, 0))
    return pl.pallas_call(
        functools.partial(_ssm_kernel, batch=batch, steps=steps, pitch=pitch),
        grid=(l // steps,),
        in_specs=[
            block,
            _const_spec(b_in_re.shape), _const_spec(b_in_im.shape),
            _const_spec(lam_re.shape), _const_spec(lam_im.shape),
            _const_spec(c_out_re.shape), _const_spec(c_out_im.shape),
            _const_spec((1, width)), _const_spec((width, width)), _const_spec((1, width)),
        ],
        out_specs=block,
        out_shape=jax.ShapeDtypeStruct((batch, l, width), BF16),
        scratch_shapes=[
            pltpu.VMEM((batch * pitch, width), F32),
            *[pltpu.VMEM((2, SSM_CHUNK_STATES // LANES, batch * pitch, LANES), F32)
              for _ in range(SSM_CHUNKS)],
            pltpu.VMEM((batch, SSM_STATES), F32), pltpu.VMEM((batch, SSM_STATES), F32),
        ],
        compiler_params=_params("arbitrary"),
        name="ssm_mixer",
    )(u, b_in_re, b_in_im, lam_re, lam_im, c_out_re, c_out_im, d_skip, w_glu, b_glu)


def _odd_in_kernel(x_ref, g_ref, w_ref, q_ref, k_ref, v_ref):
    subs = _row_subblocks(x_ref.shape[1])
    hns = [_rmsnorm(x_ref[0, rows, :], g_ref[...]).astype(BF16) for rows in subs]
    projs = [jnp.dot(hn, w_ref[...], preferred_element_type=F32) for hn in hns]
    for proj, rows in zip(projs, subs):
        q_ref[0, rows, :] = (proj[:, :SB_WIDTH] * (HEAD_DIM ** -0.5 * LOG2E)).astype(BF16)
        k_ref[0, rows, :] = proj[:, SB_WIDTH:2 * SB_WIDTH].astype(BF16)
        v_ref[0, rows, :] = proj[:, 2 * SB_WIDTH:].astype(BF16)


def _odd_in_proj(x, g, w, tm):
    b, l, d = x.shape
    row = pl.BlockSpec((1, tm, SB_WIDTH), lambda i, j: (i, j, 0))
    out = jax.ShapeDtypeStruct((b, l, SB_WIDTH), BF16)
    return pl.pallas_call(
        _odd_in_kernel,
        grid=(b, l // tm),
        in_specs=[pl.BlockSpec((1, tm, d), lambda i, j: (i, j, 0)),
                  _const_spec((1, d)), _const_spec((d, 3 * SB_WIDTH))],
        out_specs=[row, row, row],
        out_shape=[out, out, out],
        compiler_params=_params("parallel", "parallel"),
        name="odd_in_proj",
    )(x, g, w)


def _sb_streams(streams, tri):
    m = streams[0][0].shape[0]
    half = m // 2
    strict = (lax.broadcasted_iota(jnp.int32, (half, half), 1)
              < lax.broadcasted_iota(jnp.int32, (half, half), 0))

    def lower_triangle(fn, *blocks):
        quad = lambda r, c: fn(*[blk[r * half:(r + 1) * half, c * half:(c + 1) * half]
                                 for blk in blocks])
        top_left = jnp.where(strict, quad(0, 0), 0.0)
        bottom_right = jnp.where(strict, quad(1, 1), 0.0)
        top = jnp.concatenate([top_left, jnp.zeros_like(top_left)], axis=1)
        return jnp.concatenate([top, jnp.concatenate([quad(1, 0), bottom_right], axis=1)], axis=0)

    def softplus2(z):
        return jnp.maximum(z, jnp.log(1.0 + jnp.exp2(jnp.minimum(z, SB_EXP2_CLAMP))) * LOG2E)

    def scores(s):
        s["z"] = lax.dot_general(s["q"], s["k"], (((1,), (1,)), ((), ())),
                                 preferred_element_type=F32)

    def softplus(s):
        sp = lower_triangle(softplus2, s["z"]) if s["diag"] else softplus2(s["z"])
        s["sp"] = sp.astype(BF16)

    def suffix_sums(s):
        s["w"] = jnp.dot(s.pop("sp"), tri, preferred_element_type=F32)

    def weights(s):
        weight = lambda z, w: jnp.exp2(z - w)
        z = s.pop("z")
        p = lower_triangle(weight, z, s["w"]) if s["diag"] else weight(z, s["w"])
        s["p"] = p.astype(BF16)

    def values(s):
        s["pv"] = jnp.dot(s.pop("p"), s["v"], preferred_element_type=F32)

    stages = (scores, softplus, suffix_sums, weights, values)
    state = [dict(q=q, k=kh, v=vb, diag=diag) for q, kh, vb, diag in streams]
    groups = [state[i:i + SB_STREAM_GROUP] for i in range(0, len(state), SB_STREAM_GROUP)]
    for tick in range(len(groups) + len(stages) - 1):
        for g, members in enumerate(groups):
            if 0 <= tick - g < len(stages):
                for s in members:
                    stages[tick - g](s)
    return [(s["pv"], jnp.broadcast_to(s["w"][:, 0:1], (s["pv"].shape[0], LANES))) for s in state]


def _sb_kernel(q_ref, k_ref, v_ref, tri_ref, o_ref, c_ref, acc_ref, *, th, npair):
    qi = pl.program_id(2)
    lo_k = lax.broadcasted_iota(jnp.int32, (th, LANES), 1) < HEAD_DIM
    lo_q = lax.broadcasted_iota(jnp.int32, (th, LANES), 1) < HEAD_DIM
    tri = tri_ref[...]

    def load_kv(pair, j):
        rows = pl.ds(pl.multiple_of(j * th, th), th)
        lanes = slice(pair * LANES, (pair + 1) * LANES)
        kb = k_ref[0, rows, lanes]
        return (jnp.where(lo_k, kb, 0), jnp.where(lo_k, 0, kb)), v_ref[0, rows, lanes]

    def load_q(pair, half):
        return q_ref[0, half * th:(half + 1) * th, pair * LANES:(pair + 1) * LANES]

    j_diag = (2 * qi, 2 * qi + 1)
    has_left = (qi > 0).astype(F32)

    streams = []
    for pair in range(npair):
        kv_left = load_kv(pair, jnp.maximum(2 * qi - 1, 0))
        kv_diag = (load_kv(pair, j_diag[0]), load_kv(pair, j_diag[1]))
        for half in range(2):
            q = load_q(pair, half)
            (kd, vd) = kv_diag[half]
            (kl, vl) = kv_diag[0] if half == 1 else kv_left
            for h in range(2):
                streams += [(q, kd[h], vd, True), (q, kl[h], vl, False)]
    outs = iter(_sb_streams(streams, tri))
    for pair in range(npair):
        for half in range(2):
            for h in range(2):
                (pv_d, c_d), (pv_l, c_l) = next(outs), next(outs)
                scale = jnp.exp2(-c_d)
                if half == 0:
                    scale, c_l = scale * has_left, c_l * has_left
                acc_ref[pair, half, h] = pv_d + pv_l * scale
                c_ref[pair, half, h] = c_d + c_l

    def visit_rest(pair, half):
        q = load_q(pair, half)
        n_rest = jnp.maximum(j_diag[half] - 1, 0)

        def cond(carry):
            i, c_min = carry
            return jnp.logical_and(i < n_rest, c_min < SB_DEAD_LOG2)

        def body(carry):
            i, _ = carry
            ks, vb = load_kv(pair, n_rest - 1 - i)
            outs = _sb_streams([(q, ks[0], vb, False), (q, ks[1], vb, False)], tri)
            for h, (pv, c_blk) in enumerate(outs):
                c = c_ref[pair, half, h]
                acc_ref[pair, half, h] += pv * jnp.exp2(-c)
                c_ref[pair, half, h] = c + c_blk
            return i + 1, jnp.min(c_ref[pair, half])

        lax.while_loop(cond, body, (jnp.int32(0), jnp.min(c_ref[pair, half])))

    @pl.when(jnp.min(c_ref[...]) < SB_DEAD_LOG2)
    def _():
        for pair in range(npair):
            for half in range(2):
                visit_rest(pair, half)

    for pair in range(npair):
        for half in range(2):
            o_ref[0, half * th:(half + 1) * th, pair * LANES:(pair + 1) * LANES] = jnp.where(
                lo_q, acc_ref[pair, half, 0], acc_ref[pair, half, 1]).astype(BF16)


def _sb_attention(q, k, v, th, npair):
    b, l, width = q.shape
    tri = (jnp.arange(th)[:, None] >= jnp.arange(th)[None, :]).astype(BF16)
    tq = 2 * th
    lanes = npair * LANES
    state = pltpu.VMEM((npair, 2, 2, th, LANES), F32)
    return pl.pallas_call(
        functools.partial(_sb_kernel, th=th, npair=npair),
        grid=(b, width // lanes, l // tq),
        in_specs=[
            pl.BlockSpec((1, tq, lanes), lambda i, h, j: (i, j, h)),
            pl.BlockSpec((1, l, lanes), lambda i, h, j: (i, 0, h)),
            pl.BlockSpec((1, l, lanes), lambda i, h, j: (i, 0, h)),
            _const_spec((th, th)),
        ],
        out_specs=pl.BlockSpec((1, tq, lanes), lambda i, h, j: (i, j, h)),
        out_shape=jax.ShapeDtypeStruct((b, l, width), BF16),
        scratch_shapes=[state, state],
        compiler_params=_params("parallel", "parallel", "arbitrary"),
        name="sb_attention",
    )(q, k, v, tri)


def _post_kernel(*refs, n_mix):
    x_ref = refs[0]
    mix_refs = refs[1:1 + n_mix]
    wo_ref, g_ref, wg_ref, wu_ref, wd_ref, o_ref = refs[1 + n_mix:]
    subs = _row_subblocks(x_ref.shape[1])
    mixed = [jnp.concatenate([m_ref[0, rows, :] for m_ref in mix_refs], axis=-1) for rows in subs]
    xs = [x_ref[0, rows, :] + jnp.dot(m, wo_ref[...], preferred_element_type=F32)
          for m, rows in zip(mixed, subs)]
    hns = [_rmsnorm(x, g_ref[...]).astype(BF16) for x in xs]
    gates = [jnp.dot(hn, wg_ref[...], preferred_element_type=F32) for hn in hns]
    ups = [jnp.dot(hn, wu_ref[...], preferred_element_type=F32) for hn in hns]
    acts = [(gate * jax.nn.sigmoid(gate) * up).astype(BF16) for gate, up in zip(gates, ups)]
    for x, act, rows in zip(xs, acts, subs):
        o_ref[0, rows, :] = x + jnp.dot(act, wd_ref[...], preferred_element_type=F32)


def _post(x, mixes, w_out, g, w_gate, w_up, w_down, tm):
    b, l, d = x.shape
    xspec = pl.BlockSpec((1, tm, d), lambda i, j: (i, j, 0))
    mix_specs = [pl.BlockSpec((1, tm, m.shape[-1]), lambda i, j: (i, j, 0)) for m in mixes]
    return pl.pallas_call(
        functools.partial(_post_kernel, n_mix=len(mixes)),
        grid=(b, l // tm),
        in_specs=[xspec, *mix_specs, _const_spec(w_out.shape),
                  _const_spec((1, d)), _const_spec(w_gate.shape), _const_spec(w_up.shape),
                  _const_spec(w_down.shape)],
        out_specs=xspec,
        out_shape=jax.ShapeDtypeStruct(x.shape, F32),
        compiler_params=_params("parallel", "parallel"),
        name="out_proj_ffn",
    )(x, *mixes, w_out, g, w_gate, w_up, w_down)


def _tiles(l):
    tm = min(512, l)
    tm_in = min(1024, l)
    th = min(256, l // 2)
    steps = min(64, l)
    nsub = min(4, l // SWA_BLOCK)
    return tm, tm_in, th, steps, nsub


def kernel(x, even_norm, even_w_in, q_norm, k_norm, sinks, ssm_a_re, ssm_a_im, ssm_b_re, ssm_b_im,
           ssm_c_re, ssm_c_im, ssm_d, ssm_log_dt, ssm_w_glu, ssm_b_glu, even_w_out, odd_norm,
           odd_w_in, odd_w_out, ffn_norm, ffn_w_gate, ffn_w_up, ffn_w_down):
    b, l, d = x.shape
    assert d == D_MODEL and l % SWA_BLOCK == 0
    tm, tm_in, th, steps, nsub = _tiles(l)
    depth = ffn_norm.shape[0]
    bf = lambda t: t.astype(BF16)
    row = lambda t: t.reshape(1, -1)
    for layer in range(depth):
        i = layer // 2
        if layer % 2 == 0:
            q, k, v, u = _even_in_proj(
                x, row(even_norm[i]), bf(even_w_in[i]),
                row(jnp.tile(q_norm[i], 2)), row(jnp.tile(k_norm[i], 2)), tm_in)
            o_attn = _swa_attention(sinks[i], q, k, v, nsub)
            mats = _ssm_prepare(ssm_a_re[i], ssm_a_im[i], ssm_b_re[i], ssm_b_im[i],
                                ssm_c_re[i], ssm_c_im[i], ssm_log_dt[i], b)
            o_ssm = _ssm_mixer(u, mats, row(ssm_d[i]), bf(ssm_w_glu[i]), row(ssm_b_glu[i]), steps)
            mixes, w_out = [o_attn, o_ssm], bf(even_w_out[i])
        else:
            q, k, v = _odd_in_proj(x, row(odd_norm[i]), bf(odd_w_in[i]), tm_in)
            mixes, w_out = [_sb_attention(q, k, v, th, SB_PAIRS_PER_STEP)], bf(odd_w_out[i])
        x = _post(x, mixes, w_out, row(ffn_norm[layer]), bf(ffn_w_gate[layer]),
                  bf(ffn_w_up[layer]), bf(ffn_w_down[layer]), tm)
    return x
```

```python
import functools
import math

import jax
import jax.numpy as jnp
from jax import lax
from jax.experimental import pallas as pl
from jax.experimental.pallas import tpu as pltpu

F32 = jnp.float32
BF16 = jnp.bfloat16

D_MODEL = 1024
HEAD_DIM = 64
EPS = 1e-6
LANES = 128

SWA_Q_HEADS = 8
SWA_KV_HEADS = 2
SWA_BLOCK = 128
SWA_WIDTH = SWA_Q_HEADS * HEAD_DIM
KV_WIDTH = SWA_KV_HEADS * HEAD_DIM

SSM_WIDTH = D_MODEL // 2
SSM_GROUP_CH = 16
SSM_GROUPS = SSM_WIDTH // SSM_GROUP_CH
SSM_STATE = 64
SSM_STATES = SSM_GROUPS * SSM_STATE
SSM_CHUNKS = SSM_WIDTH // LANES
SSM_CHUNK_STATES = SSM_STATES // SSM_CHUNKS
EVEN_IN = SWA_WIDTH + 2 * KV_WIDTH + SSM_WIDTH

SB_HEADS = D_MODEL // HEAD_DIM
SB_WIDTH = SB_HEADS * HEAD_DIM
LOG2E = math.log2(math.e)
SB_DEAD_LOG2 = 160.0
SB_EXP2_CLAMP = 126.0
SB_PAIRS_PER_STEP = 4
SB_STREAM_GROUP = 2

VMEM_LIMIT_BYTES = 56 * 1024 * 1024
ROW_SPLIT = 2


def _params(*sem):
    return pltpu.CompilerParams(dimension_semantics=sem, vmem_limit_bytes=VMEM_LIMIT_BYTES)


def _const_spec(shape):
    nd = len(shape)
    return pl.BlockSpec(shape, lambda *_: (0,) * nd, pipeline_mode=pl.Buffered(1))


def _row_subblocks(tm):
    step = tm // ROW_SPLIT
    return [slice(i * step, (i + 1) * step) for i in range(ROW_SPLIT)]


def _rmsnorm(x, g):
    ms = jnp.mean(x * x, axis=-1, keepdims=True)
    return x * lax.rsqrt(ms + EPS) * g


def _pair_rmsnorm(x, g2):
    lo = lax.broadcasted_iota(jnp.int32, x.shape, 1) < HEAD_DIM
    sq = x * x
    s_lo = jnp.sum(jnp.where(lo, sq, 0.0), axis=-1, keepdims=True)
    s_hi = jnp.sum(jnp.where(lo, 0.0, sq), axis=-1, keepdims=True)
    ms = jnp.where(lo, s_lo, s_hi) * (1.0 / HEAD_DIM)
    return x * lax.rsqrt(ms + EPS) * g2


def _even_in_kernel(x_ref, g_ref, w_ref, qg_ref, kg_ref, q_ref, k_ref, v_ref, u_ref):
    subs = _row_subblocks(x_ref.shape[1])
    hns = [_rmsnorm(x_ref[0, rows, :], g_ref[...]).astype(BF16) for rows in subs]
    projs = [jnp.dot(hn, w_ref[...], preferred_element_type=F32) for hn in hns]
    scale = HEAD_DIM ** -0.5 * LOG2E
    for proj, rows in zip(projs, subs):
        for p in range(SWA_WIDTH // LANES):
            qp = _pair_rmsnorm(proj[:, p * LANES:(p + 1) * LANES], qg_ref[...])
            q_ref[0, rows, p * LANES:(p + 1) * LANES] = (qp * scale).astype(BF16)
        k = _pair_rmsnorm(proj[:, SWA_WIDTH:SWA_WIDTH + KV_WIDTH], kg_ref[...])
        k_ref[0, rows, :] = k.astype(BF16)
        v_ref[0, rows, :] = proj[:, SWA_WIDTH + KV_WIDTH:SWA_WIDTH + 2 * KV_WIDTH].astype(BF16)
        u_ref[0, rows, :] = proj[:, SWA_WIDTH + 2 * KV_WIDTH:]


def _even_in_proj(x, g, w, qg, kg, tm):
    b, l, d = x.shape
    return pl.pallas_call(
        _even_in_kernel,
        grid=(b, l // tm),
        in_specs=[
            pl.BlockSpec((1, tm, d), lambda i, j: (i, j, 0)),
            _const_spec((1, d)),
            _const_spec((d, EVEN_IN)),
            _const_spec((1, LANES)),
            _const_spec((1, LANES)),
        ],
        out_specs=[
            pl.BlockSpec((1, tm, SWA_WIDTH), lambda i, j: (i, j, 0)),
            pl.BlockSpec((1, tm, KV_WIDTH), lambda i, j: (i, j, 0)),
            pl.BlockSpec((1, tm, KV_WIDTH), lambda i, j: (i, j, 0)),
            pl.BlockSpec((1, tm, SSM_WIDTH), lambda i, j: (i, j, 0)),
        ],
        out_shape=[
            jax.ShapeDtypeStruct((b, l, SWA_WIDTH), BF16),
            jax.ShapeDtypeStruct((b, l, KV_WIDTH), BF16),
            jax.ShapeDtypeStruct((b, l, KV_WIDTH), BF16),
            jax.ShapeDtypeStruct((b, l, SSM_WIDTH), F32),
        ],
        compiler_params=_params("parallel", "parallel"),
        name="even_in_proj",
    )(x, g, w, qg, kg)


def _swa_kernel(sink_ref, q_ref, kc_ref, kp_ref, vc_ref, vp_ref, o_ref, *, nsub):
    n = pl.program_id(1)
    blk = SWA_BLOCK
    kall = jnp.concatenate([kp_ref[0], kc_ref[0]], axis=0).astype(F32)
    vall = jnp.concatenate([vp_ref[0], vc_ref[0]], axis=0).astype(F32)
    lo = lax.broadcasted_iota(jnp.int32, kall.shape, 1) < HEAD_DIM

    def halves(t):
        g0_lo = jnp.where(lo, t, 0.0)
        g1_hi = jnp.where(lo, 0.0, t)
        g0_hi = pltpu.roll(g0_lo, HEAD_DIM, 1)
        g1_lo = pltpu.roll(g1_hi, HEAD_DIM, 1)
        return [[g0_lo.astype(BF16), g0_hi.astype(BF16)], [g1_lo.astype(BF16), g1_hi.astype(BF16)]]

    ks, vs = halves(kall), halves(vall)
    qq = lax.broadcasted_iota(jnp.int32, (blk, 2 * blk), 0)
    kk = lax.broadcasted_iota(jnp.int32, (blk, 2 * blk), 1)
    diff = qq + blk - kk
    band = (diff >= 0) & (diff < blk)
    first_key = jnp.where(n > 0, 0, blk)
    band_first = band & (kk >= first_key)

    streams = []
    for s in range(nsub):
        keys = slice(s * blk, (s + 2) * blk)
        for p in range(SWA_WIDTH // LANES):
            g = (2 * p) // (SWA_Q_HEADS // SWA_KV_HEADS)
            for e in range(2):
                streams.append(dict(
                    q=q_ref[0, s * blk:(s + 1) * blk, p * LANES:(p + 1) * LANES],
                    k=ks[g][e][keys], v=vs[g][e][keys],
                    sink=sink_ref[2 * p + e] * LOG2E, mask=band_first if s == 0 else band))
    group = len(streams) // nsub
    for g0 in range(0, len(streams), group):
        for st in streams[g0:g0 + group]:
            st["z"] = lax.dot_general(st["q"], st["k"], (((1,), (1,)), ((), ())),
                                      preferred_element_type=F32)
        for st in streams[g0:g0 + group]:
            z = jnp.where(st["mask"], st.pop("z"), -jnp.inf)
            m = jnp.maximum(jnp.max(z, axis=-1, keepdims=True), st["sink"])
            pe = jnp.exp2(z - m)
            st["den"] = jnp.sum(pe, axis=-1, keepdims=True) + jnp.exp2(st["sink"] - m)
            st["pe"] = pe.astype(BF16)
        for st in streams[g0:g0 + group]:
            st["o"] = jnp.dot(st.pop("pe"), st["v"], preferred_element_type=F32) / st["den"]
    for s in range(nsub):
        for p in range(SWA_WIDTH // LANES):
            i = (s * (SWA_WIDTH // LANES) + p) * 2
            o_ref[0, s * blk:(s + 1) * blk, p * LANES:(p + 1) * LANES] = (
                streams[i]["o"] + streams[i + 1]["o"]).astype(BF16)


def _swa_attention(sinks, q, k, v, nsub):
    b, l, _ = q.shape
    blk = SWA_BLOCK * nsub
    cur = lambda i, j: (i, j, 0)
    prev = lambda i, j: (i, jnp.maximum(j * nsub - 1, 0), 0)
    return pl.pallas_call(
        functools.partial(_swa_kernel, nsub=nsub),
        grid=(b, l // blk),
        in_specs=[
            pl.BlockSpec(memory_space=pltpu.SMEM),
            pl.BlockSpec((1, blk, SWA_WIDTH), cur),
            pl.BlockSpec((1, blk, KV_WIDTH), cur),
            pl.BlockSpec((1, SWA_BLOCK, KV_WIDTH), prev),
            pl.BlockSpec((1, blk, KV_WIDTH), cur),
            pl.BlockSpec((1, SWA_BLOCK, KV_WIDTH), prev),
        ],
        out_specs=pl.BlockSpec((1, blk, SWA_WIDTH), cur),
        out_shape=jax.ShapeDtypeStruct((b, l, SWA_WIDTH), BF16),
        compiler_params=_params("parallel", "parallel"),
        name="swa_attention",
    )(sinks, q, k, k, v, v)


def _ssm_disc_kernel(are_ref, aim_ref, ldt_ref, lre_ref, lim_ref, wre_ref, wim_ref):
    a_re, a_im = are_ref[...], aim_ref[...]
    dt = jnp.exp(ldt_ref[...])
    mag = jnp.exp(a_re * dt)
    lam_re = mag * jnp.cos(a_im * dt)
    lam_im = mag * jnp.sin(a_im * dt)
    den = a_re * a_re + a_im * a_im
    lre_ref[...] = lam_re
    lim_ref[...] = lam_im
    wre_ref[...] = ((lam_re - 1.0) * a_re + lam_im * a_im) / den
    wim_ref[...] = (lam_im * a_re - (lam_re - 1.0) * a_im) / den


def _ssm_bbar_kernel(wre_ref, wim_ref, bre_ref, bim_ref, ore_ref, oim_ref):
    w_re, w_im, b_re, b_im = wre_ref[...], wim_ref[...], bre_ref[...], bim_ref[...]
    ore_ref[...] = w_re * b_re - w_im * b_im
    oim_ref[...] = w_re * b_im + w_im * b_re


def _block_diag(t):
    ch, gl, r, c = t.shape
    eye = jnp.eye(gl, dtype=t.dtype)
    return (t[:, :, :, None, :] * eye[None, :, None, :, None]).reshape(ch, gl * r, gl * c)


def _ssm_prepare(a_re, a_im, b_re, b_im, c_re, c_im, log_dt, batch):
    g, n, p = SSM_GROUPS, SSM_STATE, SSM_GROUP_CH
    gn = jax.ShapeDtypeStruct((g, n), F32)
    ldt = jnp.broadcast_to(log_dt[:, None], (g, n))
    lam_re, lam_im, w_re, w_im = pl.pallas_call(
        _ssm_disc_kernel, out_shape=[gn, gn, gn, gn], name="ssm_discretise")(a_re, a_im, ldt)
    gnp = jax.ShapeDtypeStruct((g, n * p), F32)
    bb_re, bb_im = pl.pallas_call(_ssm_bbar_kernel, out_shape=[gnp, gnp], name="ssm_bbar")(
        jnp.repeat(w_re, p, axis=1), jnp.repeat(w_im, p, axis=1),
        b_re.reshape(g, n * p), b_im.reshape(g, n * p))
    gl = g // SSM_CHUNKS
    to_in = lambda t: _block_diag(
        t.reshape(SSM_CHUNKS, gl, n, p).transpose(0, 1, 3, 2)).astype(BF16)
    to_out = lambda t: _block_diag(
        t.reshape(SSM_CHUNKS, gl, p, n).transpose(0, 1, 3, 2)).astype(BF16)
    bcast = lambda t: jnp.broadcast_to(t.reshape(1, SSM_STATES), (batch, SSM_STATES))
    return (to_in(bb_re), to_in(bb_im), bcast(lam_re), bcast(lam_im), to_out(c_re), to_out(c_im))


def _ssm_kernel(u_ref, bre_ref, bim_ref, lre_ref, lim_ref, cre_ref, cim_ref, d_ref, wg_ref, bg_ref,
                o_ref, ubuf_ref, h0_ref, h1_ref, h2_ref, h3_ref, sre_ref, sim_ref,
                *, batch, steps, pitch):
    @pl.when(pl.program_id(0) == 0)
    def _():
        sre_ref[...] = jnp.zeros_like(sre_ref)
        sim_ref[...] = jnp.zeros_like(sim_ref)
        ubuf_ref[...] = jnp.zeros_like(ubuf_ref)

    for b in range(batch):
        ubuf_ref[b * pitch:b * pitch + steps, :] = u_ref[b]
    u = ubuf_ref[...]
    ub = u.astype(BF16)
    tiles = SSM_CHUNK_STATES // LANES
    ys = [None] * SSM_CHUNKS
    h_refs = (h0_ref, h1_ref, h2_ref, h3_ref)

    def input_matmul(c, part):
        w_ref = (bre_ref, bim_ref)[part]
        bu = jnp.dot(ub[:, c * LANES:(c + 1) * LANES], w_ref[c], preferred_element_type=F32)
        for i in range(tiles):
            h_refs[c][part, i] = bu[:, i * LANES:(i + 1) * LANES]

    def output_matmul(c):
        hs = [jnp.concatenate([h_refs[c][part, i] for i in range(tiles)], axis=1).astype(BF16)
              for part in range(2)]
        ys[c] = (jnp.dot(hs[0], cre_ref[c], preferred_element_type=F32)
                 - jnp.dot(hs[1], cim_ref[c], preferred_element_type=F32))

    def scan(chunks, fillers):
        slabs = [(c, i) for c in chunks for i in range(tiles)]
        lanes = [slice((c * tiles + i) * LANES, (c * tiles + i + 1) * LANES) for c, i in slabs]
        h_re = [sre_ref[:, ln] for ln in lanes]
        h_im = [sim_ref[:, ln] for ln in lanes]
        l_re = [lre_ref[:, ln] for ln in lanes]
        l_im = [lim_ref[:, ln] for ln in lanes]
        fillers = list(fillers)
        every = max(1, steps // max(1, len(fillers)))
        for t in range(steps):
            rows = pl.ds(t, batch, stride=pitch)
            for i, (c, tile) in enumerate(slabs):
                n_re = l_re[i] * h_re[i] - l_im[i] * h_im[i] + h_refs[c][0, tile, rows, :]
                n_im = l_re[i] * h_im[i] + l_im[i] * h_re[i] + h_refs[c][1, tile, rows, :]
                h_refs[c][0, tile, rows, :] = n_re
                h_refs[c][1, tile, rows, :] = n_im
                h_re[i], h_im[i] = n_re, n_im
            if fillers and (t + 1) % every == 0:
                fillers.pop(0)()
        for f in fillers:
            f()
        for i, ln in enumerate(lanes):
            sre_ref[:, ln] = h_re[i]
            sim_ref[:, ln] = h_im[i]

    first, second = (0, 1), (2, 3)
    for c in first:
        input_matmul(c, 0)
        input_matmul(c, 1)
    scan(first, [functools.partial(input_matmul, c, part) for c in second for part in range(2)])
    scan(second, [functools.partial(output_matmul, c) for c in first])
    for c in second:
        output_matmul(c)
    y = jnp.concatenate(ys, axis=-1) + d_ref[...] * u
    y = jax.nn.gelu(y)
    gate = jnp.dot(y.astype(BF16), wg_ref[...], preferred_element_type=F32) + bg_ref[...]
    out = y * jax.nn.sigmoid(gate)
    for b in range(batch):
        o_ref[b] = out[b * pitch:b * pitch + steps].astype(BF16)


def _ssm_mixer(u, mats, d_skip, w_glu, b_glu, steps):
    batch, l, width = u.shape
    assert steps % 8 == 0
    pitch = steps + 4
    b_in_re, b_in_im, lam_re, lam_im, c_out_re, c_out_im = mats
    block = pl.BlockSpec((batch, steps, width), lambda i: (0, i, 0))
    return pl.pallas_call(
        functools.partial(_ssm_kernel, batch=batch, steps=steps, pitch=pitch),
        grid=(l // steps,),
        in_specs=[
            block,
            _const_spec(b_in_re.shape), _const_spec(b_in_im.shape),
            _const_spec(lam_re.shape), _const_spec(lam_im.shape),
            _const_spec(c_out_re.shape), _const_spec(c_out_im.shape),
            _const_spec((1, width)), _const_spec((width, width)), _const_spec((1, width)),
        ],
        out_specs=block,
        out_shape=jax.ShapeDtypeStruct((batch, l, width), BF16),
        scratch_shapes=[
            pltpu.VMEM((batch * pitch, width), F32),
            *[pltpu.VMEM((2, SSM_CHUNK_STATES // LANES, batch * pitch, LANES), F32)
              for _ in range(SSM_CHUNKS)],
            pltpu.VMEM((batch, SSM_STATES), F32), pltpu.VMEM((batch, SSM_STATES), F32),
        ],
        compiler_params=_params("arbitrary"),
        name="ssm_mixer",
    )(u, b_in_re, b_in_im, lam_re, lam_im, c_out_re, c_out_im, d_skip, w_glu, b_glu)


def _odd_in_kernel(x_ref, g_ref, w_ref, q_ref, k_ref, v_ref):
    subs = _row_subblocks(x_ref.shape[1])
    hns = [_rmsnorm(x_ref[0, rows, :], g_ref[...]).astype(BF16) for rows in subs]
    projs = [jnp.dot(hn, w_ref[...], preferred_element_type=F32) for hn in hns]
    for proj, rows in zip(projs, subs):
        q_ref[0, rows, :] = (proj[:, :SB_WIDTH] * (HEAD_DIM ** -0.5 * LOG2E)).astype(BF16)
        k_ref[0, rows, :] = proj[:, SB_WIDTH:2 * SB_WIDTH].astype(BF16)
        v_ref[0, rows, :] = proj[:, 2 * SB_WIDTH:].astype(BF16)


def _odd_in_proj(x, g, w, tm):
    b, l, d = x.shape
    row = pl.BlockSpec((1, tm, SB_WIDTH), lambda i, j: (i, j, 0))
    out = jax.ShapeDtypeStruct((b, l, SB_WIDTH), BF16)
    return pl.pallas_call(
        _odd_in_kernel,
        grid=(b, l // tm),
        in_specs=[pl.BlockSpec((1, tm, d), lambda i, j: (i, j, 0)),
                  _const_spec((1, d)), _const_spec((d, 3 * SB_WIDTH))],
        out_specs=[row, row, row],
        out_shape=[out, out, out],
        compiler_params=_params("parallel", "parallel"),
        name="odd_in_proj",
    )(x, g, w)


def _sb_streams(streams, tri):
    m = streams[0][0].shape[0]
    half = m // 2
    strict = (lax.broadcasted_iota(jnp.int32, (half, half), 1)
              < lax.broadcasted_iota(jnp.int32, (half, half), 0))

    def lower_triangle(fn, *blocks):
        quad = lambda r, c: fn(*[blk[r * half:(r + 1) * half, c * half:(c + 1) * half]
                                 for blk in blocks])
        top_left = jnp.where(strict, quad(0, 0), 0.0)
        bottom_right = jnp.where(strict, quad(1, 1), 0.0)
        top = jnp.concatenate([top_left, jnp.zeros_like(top_left)], axis=1)
        return jnp.concatenate([top, jnp.concatenate([quad(1, 0), bottom_right], axis=1)], axis=0)

    def softplus2(z):
        return jnp.maximum(z, jnp.log(1.0 + jnp.exp2(jnp.minimum(z, SB_EXP2_CLAMP))) * LOG2E)

    def scores(s):
        s["z"] = lax.dot_general(s["q"], s["k"], (((1,), (1,)), ((), ())),
                                 preferred_element_type=F32)

    def softplus(s):
        sp = lower_triangle(softplus2, s["z"]) if s["diag"] else softplus2(s["z"])
        s["sp"] = sp.astype(BF16)

    def suffix_sums(s):
        s["w"] = jnp.dot(s.pop("sp"), tri, preferred_element_type=F32)

    def weights(s):
        weight = lambda z, w: jnp.exp2(z - w)
        z = s.pop("z")
        p = lower_triangle(weight, z, s["w"]) if s["diag"] else weight(z, s["w"])
        s["p"] = p.astype(BF16)

    def values(s):
        s["pv"] = jnp.dot(s.pop("p"), s["v"], preferred_element_type=F32)

    stages = (scores, softplus, suffix_sums, weights, values)
    state = [dict(q=q, k=kh, v=vb, diag=diag) for q, kh, vb, diag in streams]
    groups = [state[i:i + SB_STREAM_GROUP] for i in range(0, len(state), SB_STREAM_GROUP)]
    for tick in range(len(groups) + len(stages) - 1):
        for g, members in enumerate(groups):
            if 0 <= tick - g < len(stages):
                for s in members:
                    stages[tick - g](s)
    return [(s["pv"], jnp.broadcast_to(s["w"][:, 0:1], (s["pv"].shape[0], LANES))) for s in state]


def _sb_kernel(q_ref, k_ref, v_ref, tri_ref, o_ref, c_ref, acc_ref, *, th, npair):
    qi = pl.program_id(2)
    lo_k = lax.broadcasted_iota(jnp.int32, (th, LANES), 1) < HEAD_DIM
    lo_q = lax.broadcasted_iota(jnp.int32, (th, LANES), 1) < HEAD_DIM
    tri = tri_ref[...]

    def load_kv(pair, j):
        rows = pl.ds(pl.multiple_of(j * th, th), th)
        lanes = slice(pair * LANES, (pair + 1) * LANES)
        kb = k_ref[0, rows, lanes]
        return (jnp.where(lo_k, kb, 0), jnp.where(lo_k, 0, kb)), v_ref[0, rows, lanes]

    def load_q(pair, half):
        return q_ref[0, half * th:(half + 1) * th, pair * LANES:(pair + 1) * LANES]

    j_diag = (2 * qi, 2 * qi + 1)
    has_left = (qi > 0).astype(F32)

    streams = []
    for pair in range(npair):
        kv_left = load_kv(pair, jnp.maximum(2 * qi - 1, 0))
        kv_diag = (load_kv(pair, j_diag[0]), load_kv(pair, j_diag[1]))
        for half in range(2):
            q = load_q(pair, half)
            (kd, vd) = kv_diag[half]
            (kl, vl) = kv_diag[0] if half == 1 else kv_left
            for h in range(2):
                streams += [(q, kd[h], vd, True), (q, kl[h], vl, False)]
    outs = iter(_sb_streams(streams, tri))
    for pair in range(npair):
        for half in range(2):
            for h in range(2):
                (pv_d, c_d), (pv_l, c_l) = next(outs), next(outs)
                scale = jnp.exp2(-c_d)
                if half == 0:
                    scale, c_l = scale * has_left, c_l * has_left
                acc_ref[pair, half, h] = pv_d + pv_l * scale
                c_ref[pair, half, h] = c_d + c_l

    def visit_rest(pair, half):
        q = load_q(pair, half)
        n_rest = jnp.maximum(j_diag[half] - 1, 0)

        def cond(carry):
            i, c_min = carry
            return jnp.logical_and(i < n_rest, c_min < SB_DEAD_LOG2)

        def body(carry):
            i, _ = carry
            ks, vb = load_kv(pair, n_rest - 1 - i)
            outs = _sb_streams([(q, ks[0], vb, False), (q, ks[1], vb, False)], tri)
            for h, (pv, c_blk) in enumerate(outs):
                c = c_ref[pair, half, h]
                acc_ref[pair, half, h] += pv * jnp.exp2(-c)
                c_ref[pair, half, h] = c + c_blk
            return i + 1, jnp.min(c_ref[pair, half])

        lax.while_loop(cond, body, (jnp.int32(0), jnp.min(c_ref[pair, half])))

    @pl.when(jnp.min(c_ref[...]) < SB_DEAD_LOG2)
    def _():
        for pair in range(npair):
            for half in range(2):
                visit_rest(pair, half)

    for pair in range(npair):
        for half in range(2):
            o_ref[0, half * th:(half + 1) * th, pair * LANES:(pair + 1) * LANES] = jnp.where(
                lo_q, acc_ref[pair, half, 0], acc_ref[pair, half, 1]).astype(BF16)


def _sb_attention(q, k, v, th, npair):
    b, l, width = q.shape
    tri = (jnp.arange(th)[:, None] >= jnp.arange(th)[None, :]).astype(BF16)
    tq = 2 * th
    lanes = npair * LANES
    state = pltpu.VMEM((npair, 2, 2, th, LANES), F32)
    return pl.pallas_call(
        functools.partial(_sb_kernel, th=th, npair=npair),
        grid=(b, width // lanes, l // tq),
        in_specs=[
            pl.BlockSpec((1, tq, lanes), lambda i, h, j: (i, j, h)),
            pl.BlockSpec((1, l, lanes), lambda i, h, j: (i, 0, h)),
            pl.BlockSpec((1, l, lanes), lambda i, h, j: (i, 0, h)),
            _const_spec((th, th)),
        ],
        out_specs=pl.BlockSpec((1, tq, lanes), lambda i, h, j: (i, j, h)),
        out_shape=jax.ShapeDtypeStruct((b, l, width), BF16),
        scratch_shapes=[state, state],
        compiler_params=_params("parallel", "parallel", "arbitrary"),
        name="sb_attention",
    )(q, k, v, tri)


def _post_kernel(*refs, n_mix):
    x_ref = refs[0]
    mix_refs = refs[1:1 + n_mix]
    wo_ref, g_ref, wg_ref, wu_ref, wd_ref, o_ref = refs[1 + n_mix:]
    subs = _row_subblocks(x_ref.shape[1])
    mixed = [jnp.concatenate([m_ref[0, rows, :] for m_ref in mix_refs], axis=-1) for rows in subs]
    xs = [x_ref[0, rows, :] + jnp.dot(m, wo_ref[...], preferred_element_type=F32)
          for m, rows in zip(mixed, subs)]
    hns = [_rmsnorm(x, g_ref[...]).astype(BF16) for x in xs]
    gates = [jnp.dot(hn, wg_ref[...], preferred_element_type=F32) for hn in hns]
    ups = [jnp.dot(hn, wu_ref[...], preferred_element_type=F32) for hn in hns]
    acts = [(gate * jax.nn.sigmoid(gate) * up).astype(BF16) for gate, up in zip(gates, ups)]
    for x, act, rows in zip(xs, acts, subs):
        o_ref[0, rows, :] = x + jnp.dot(act, wd_ref[...], preferred_element_type=F32)


def _post(x, mixes, w_out, g, w_gate, w_up, w_down, tm):
    b, l, d = x.shape
    xspec = pl.BlockSpec((1, tm, d), lambda i, j: (i, j, 0))
    mix_specs = [pl.BlockSpec((1, tm, m.shape[-1]), lambda i, j: (i, j, 0)) for m in mixes]
    return pl.pallas_call(
        functools.partial(_post_kernel, n_mix=len(mixes)),
        grid=(b, l // tm),
        in_specs=[xspec, *mix_specs, _const_spec(w_out.shape),
                  _const_spec((1, d)), _const_spec(w_gate.shape), _const_spec(w_up.shape),
                  _const_spec(w_down.shape)],
        out_specs=xspec,
        out_shape=jax.ShapeDtypeStruct(x.shape, F32),
        compiler_params=_params("parallel", "parallel"),
        name="out_proj_ffn",
    )(x, *mixes, w_out, g, w_gate, w_up, w_down)


def _tiles(l):
    tm = min(512, l)
    tm_in = min(1024, l)
    th = min(256, l // 2)
    steps = min(128, l)
    nsub = min(8, l // SWA_BLOCK)
    return tm, tm_in, th, steps, nsub


def kernel(x, even_norm, even_w_in, q_norm, k_norm, sinks, ssm_a_re, ssm_a_im, ssm_b_re, ssm_b_im,
           ssm_c_re, ssm_c_im, ssm_d, ssm_log_dt, ssm_w_glu, ssm_b_glu, even_w_out, odd_norm,
           odd_w_in, odd_w_out, ffn_norm, ffn_w_gate, ffn_w_up, ffn_w_down):
    b, l, d = x.shape
    assert d == D_MODEL and l % SWA_BLOCK == 0
    tm, tm_in, th, steps, nsub = _tiles(l)
    depth = ffn_norm.shape[0]
    row = lambda t: t.reshape(1, -1)
    (even_w_in, ssm_w_glu, even_w_out, odd_w_in, odd_w_out, ffn_w_gate, ffn_w_up, ffn_w_down) = (
        t.astype(BF16) for t in (even_w_in, ssm_w_glu, even_w_out, odd_w_in, odd_w_out,
                                 ffn_w_gate, ffn_w_up, ffn_w_down))
    for layer in range(depth):
        i = layer // 2
        if layer % 2 == 0:
            q, k, v, u = _even_in_proj(
                x, row(even_norm[i]), even_w_in[i],
                row(jnp.tile(q_norm[i], 2)), row(jnp.tile(k_norm[i], 2)), tm_in)
            o_attn = _swa_attention(sinks[i], q, k, v, nsub)
            mats = _ssm_prepare(ssm_a_re[i], ssm_a_im[i], ssm_b_re[i], ssm_b_im[i],
                                ssm_c_re[i], ssm_c_im[i], ssm_log_dt[i], b)
            o_ssm = _ssm_mixer(u, mats, row(ssm_d[i]), ssm_w_glu[i], row(ssm_b_glu[i]), steps)
            mixes, w_out = [o_attn, o_ssm], even_w_out[i]
        else:
            q, k, v = _odd_in_proj(x, row(odd_norm[i]), odd_w_in[i], tm_in)
            mixes, w_out = [_sb_attention(q, k, v, th, SB_PAIRS_PER_STEP)], odd_w_out[i]
        x = _post(x, mixes, w_out, row(ffn_norm[layer]), ffn_w_gate[layer],
                  ffn_w_up[layer], ffn_w_down[layer], tm)
    return x
```

```python
import functools
import math

import jax
import jax.numpy as jnp
from jax import lax
from jax.experimental import pallas as pl
from jax.experimental.pallas import tpu as pltpu

F32 = jnp.float32
BF16 = jnp.bfloat16

D_MODEL = 1024
HEAD_DIM = 64
EPS = 1e-6
LANES = 128

SWA_Q_HEADS = 8
SWA_KV_HEADS = 2
SWA_BLOCK = 128
SWA_WIDTH = SWA_Q_HEADS * HEAD_DIM
KV_WIDTH = SWA_KV_HEADS * HEAD_DIM

SSM_WIDTH = D_MODEL // 2
SSM_GROUP_CH = 16
SSM_GROUPS = SSM_WIDTH // SSM_GROUP_CH
SSM_STATE = 64
SSM_STATES = SSM_GROUPS * SSM_STATE
SSM_CHUNKS = SSM_WIDTH // LANES
SSM_CHUNK_STATES = SSM_STATES // SSM_CHUNKS
EVEN_IN = SWA_WIDTH + 2 * KV_WIDTH + SSM_WIDTH

SB_HEADS = D_MODEL // HEAD_DIM
SB_WIDTH = SB_HEADS * HEAD_DIM
LOG2E = math.log2(math.e)
SB_DEAD_LOG2 = 160.0
SB_EXP2_CLAMP = 126.0
SB_PAIRS_PER_STEP = 4
SB_STREAM_GROUP = 2

VMEM_LIMIT_BYTES = 56 * 1024 * 1024
ROW_SPLIT = 2


def _params(*sem):
    return pltpu.CompilerParams(dimension_semantics=sem, vmem_limit_bytes=VMEM_LIMIT_BYTES)


def _const_spec(shape):
    nd = len(shape)
    return pl.BlockSpec(shape, lambda *_: (0,) * nd, pipeline_mode=pl.Buffered(1))


def _layer_spec(stacked, layer):
    rest = stacked.shape[1:]
    return pl.BlockSpec((None, *rest), lambda *_: (layer,) + (0,) * len(rest),
                        pipeline_mode=pl.Buffered(1))


def _row_subblocks(tm):
    step = tm // ROW_SPLIT
    return [slice(i * step, (i + 1) * step) for i in range(ROW_SPLIT)]


def _rmsnorm(x, g):
    ms = jnp.mean(x * x, axis=-1, keepdims=True)
    return x * lax.rsqrt(ms + EPS) * g


def _pair_rmsnorm(x, g2):
    lo = lax.broadcasted_iota(jnp.int32, x.shape, 1) < HEAD_DIM
    sq = x * x
    s_lo = jnp.sum(jnp.where(lo, sq, 0.0), axis=-1, keepdims=True)
    s_hi = jnp.sum(jnp.where(lo, 0.0, sq), axis=-1, keepdims=True)
    ms = jnp.where(lo, s_lo, s_hi) * (1.0 / HEAD_DIM)
    return x * lax.rsqrt(ms + EPS) * g2


def _even_in_kernel(x_ref, g_ref, w_ref, qg_ref, kg_ref, q_ref, k_ref, v_ref, u_ref):
    subs = _row_subblocks(x_ref.shape[1])
    hns = [_rmsnorm(x_ref[0, rows, :], g_ref[...]).astype(BF16) for rows in subs]
    projs = [jnp.dot(hn, w_ref[...], preferred_element_type=F32) for hn in hns]
    scale = HEAD_DIM ** -0.5 * LOG2E
    for proj, rows in zip(projs, subs):
        for p in range(SWA_WIDTH // LANES):
            qp = _pair_rmsnorm(proj[:, p * LANES:(p + 1) * LANES], qg_ref[...])
            q_ref[0, rows, p * LANES:(p + 1) * LANES] = (qp * scale).astype(BF16)
        k = _pair_rmsnorm(proj[:, SWA_WIDTH:SWA_WIDTH + KV_WIDTH], kg_ref[...])
        k_ref[0, rows, :] = k.astype(BF16)
        v_ref[0, rows, :] = proj[:, SWA_WIDTH + KV_WIDTH:SWA_WIDTH + 2 * KV_WIDTH].astype(BF16)
        u_ref[0, rows, :] = proj[:, SWA_WIDTH + 2 * KV_WIDTH:]


def _even_in_proj(x, g, w, layer, qg, kg, tm):
    b, l, d = x.shape
    return pl.pallas_call(
        _even_in_kernel,
        grid=(b, l // tm),
        in_specs=[
            pl.BlockSpec((1, tm, d), lambda i, j: (i, j, 0)),
            _const_spec((1, d)),
            _layer_spec(w, layer),
            _const_spec((1, LANES)),
            _const_spec((1, LANES)),
        ],
        out_specs=[
            pl.BlockSpec((1, tm, SWA_WIDTH), lambda i, j: (i, j, 0)),
            pl.BlockSpec((1, tm, KV_WIDTH), lambda i, j: (i, j, 0)),
            pl.BlockSpec((1, tm, KV_WIDTH), lambda i, j: (i, j, 0)),
            pl.BlockSpec((1, tm, SSM_WIDTH), lambda i, j: (i, j, 0)),
        ],
        out_shape=[
            jax.ShapeDtypeStruct((b, l, SWA_WIDTH), BF16),
            jax.ShapeDtypeStruct((b, l, KV_WIDTH), BF16),
            jax.ShapeDtypeStruct((b, l, KV_WIDTH), BF16),
            jax.ShapeDtypeStruct((b, l, SSM_WIDTH), F32),
        ],
        compiler_params=_params("parallel", "parallel"),
        name="even_in_proj",
    )(x, g, w, qg, kg)


def _swa_kernel(sink_ref, q_ref, kc_ref, kp_ref, vc_ref, vp_ref, o_ref, *, nsub):
    n = pl.program_id(1)
    blk = SWA_BLOCK
    kall = jnp.concatenate([kp_ref[0], kc_ref[0]], axis=0).astype(F32)
    vall = jnp.concatenate([vp_ref[0], vc_ref[0]], axis=0).astype(F32)
    lo = lax.broadcasted_iota(jnp.int32, kall.shape, 1) < HEAD_DIM

    def halves(t):
        g0_lo = jnp.where(lo, t, 0.0)
        g1_hi = jnp.where(lo, 0.0, t)
        g0_hi = pltpu.roll(g0_lo, HEAD_DIM, 1)
        g1_lo = pltpu.roll(g1_hi, HEAD_DIM, 1)
        return [[g0_lo.astype(BF16), g0_hi.astype(BF16)], [g1_lo.astype(BF16), g1_hi.astype(BF16)]]

    ks, vs = halves(kall), halves(vall)
    qq = lax.broadcasted_iota(jnp.int32, (blk, 2 * blk), 0)
    kk = lax.broadcasted_iota(jnp.int32, (blk, 2 * blk), 1)
    diff = qq + blk - kk
    band = (diff >= 0) & (diff < blk)
    first_key = jnp.where(n > 0, 0, blk)
    band_first = band & (kk >= first_key)

    streams = []
    for s in range(nsub):
        keys = slice(s * blk, (s + 2) * blk)
        for p in range(SWA_WIDTH // LANES):
            g = (2 * p) // (SWA_Q_HEADS // SWA_KV_HEADS)
            for e in range(2):
                streams.append(dict(
                    q=q_ref[0, s * blk:(s + 1) * blk, p * LANES:(p + 1) * LANES],
                    k=ks[g][e][keys], v=vs[g][e][keys],
                    sink=sink_ref[2 * p + e] * LOG2E, mask=band_first if s == 0 else band))
    group = len(streams) // nsub
    for g0 in range(0, len(streams), group):
        for st in streams[g0:g0 + group]:
            st["z"] = lax.dot_general(st["q"], st["k"], (((1,), (1,)), ((), ())),
                                      preferred_element_type=F32)
        for st in streams[g0:g0 + group]:
            z = jnp.where(st["mask"], st.pop("z"), -jnp.inf)
            m = jnp.maximum(jnp.max(z, axis=-1, keepdims=True), st["sink"])
            pe = jnp.exp2(z - m)
            st["den"] = jnp.sum(pe, axis=-1, keepdims=True) + jnp.exp2(st["sink"] - m)
            st["pe"] = pe.astype(BF16)
        for st in streams[g0:g0 + group]:
            st["o"] = jnp.dot(st.pop("pe"), st["v"], preferred_element_type=F32) / st["den"]
    for s in range(nsub):
        for p in range(SWA_WIDTH // LANES):
            i = (s * (SWA_WIDTH // LANES) + p) * 2
            o_ref[0, s * blk:(s + 1) * blk, p * LANES:(p + 1) * LANES] = (
                streams[i]["o"] + streams[i + 1]["o"]).astype(BF16)


def _swa_attention(sinks, q, k, v, nsub):
    b, l, _ = q.shape
    blk = SWA_BLOCK * nsub
    cur = lambda i, j: (i, j, 0)
    prev = lambda i, j: (i, jnp.maximum(j * nsub - 1, 0), 0)
    return pl.pallas_call(
        functools.partial(_swa_kernel, nsub=nsub),
        grid=(b, l // blk),
        in_specs=[
            pl.BlockSpec(memory_space=pltpu.SMEM),
            pl.BlockSpec((1, blk, SWA_WIDTH), cur),
            pl.BlockSpec((1, blk, KV_WIDTH), cur),
            pl.BlockSpec((1, SWA_BLOCK, KV_WIDTH), prev),
            pl.BlockSpec((1, blk, KV_WIDTH), cur),
            pl.BlockSpec((1, SWA_BLOCK, KV_WIDTH), prev),
        ],
        out_specs=pl.BlockSpec((1, blk, SWA_WIDTH), cur),
        out_shape=jax.ShapeDtypeStruct((b, l, SWA_WIDTH), BF16),
        compiler_params=_params("parallel", "parallel"),
        name="swa_attention",
    )(sinks, q, k, k, v, v)


def _ssm_disc_kernel(are_ref, aim_ref, ldt_ref, lre_ref, lim_ref, wre_ref, wim_ref):
    a_re, a_im = are_ref[...], aim_ref[...]
    dt = jnp.exp(ldt_ref[...])
    mag = jnp.exp(a_re * dt)
    lam_re = mag * jnp.cos(a_im * dt)
    lam_im = mag * jnp.sin(a_im * dt)
    den = a_re * a_re + a_im * a_im
    lre_ref[...] = lam_re
    lim_ref[...] = lam_im
    wre_ref[...] = ((lam_re - 1.0) * a_re + lam_im * a_im) / den
    wim_ref[...] = (lam_im * a_re - (lam_re - 1.0) * a_im) / den


def _ssm_bbar_kernel(wre_ref, wim_ref, bre_ref, bim_ref, ore_ref, oim_ref):
    w_re, w_im, b_re, b_im = wre_ref[...], wim_ref[...], bre_ref[...], bim_ref[...]
    ore_ref[...] = w_re * b_re - w_im * b_im
    oim_ref[...] = w_re * b_im + w_im * b_re


def _block_diag(t):
    ch, gl, r, c = t.shape
    eye = jnp.eye(gl, dtype=t.dtype)
    return (t[:, :, :, None, :] * eye[None, :, None, :, None]).reshape(ch, gl * r, gl * c)


def _ssm_prepare(a_re, a_im, b_re, b_im, c_re, c_im, log_dt, batch):
    g, n, p = SSM_GROUPS, SSM_STATE, SSM_GROUP_CH
    gn = jax.ShapeDtypeStruct((g, n), F32)
    ldt = jnp.broadcast_to(log_dt[:, None], (g, n))
    lam_re, lam_im, w_re, w_im = pl.pallas_call(
        _ssm_disc_kernel, out_shape=[gn, gn, gn, gn], name="ssm_discretise")(a_re, a_im, ldt)
    gnp = jax.ShapeDtypeStruct((g, n * p), F32)
    bb_re, bb_im = pl.pallas_call(_ssm_bbar_kernel, out_shape=[gnp, gnp], name="ssm_bbar")(
        jnp.repeat(w_re, p, axis=1), jnp.repeat(w_im, p, axis=1),
        b_re.reshape(g, n * p), b_im.reshape(g, n * p))
    gl = g // SSM_CHUNKS
    to_in = lambda t: _block_diag(
        t.reshape(SSM_CHUNKS, gl, n, p).transpose(0, 1, 3, 2)).astype(BF16)
    to_out = lambda t: _block_diag(
        t.reshape(SSM_CHUNKS, gl, p, n).transpose(0, 1, 3, 2)).astype(BF16)
    bcast = lambda t: jnp.broadcast_to(t.reshape(1, SSM_STATES), (batch, SSM_STATES))
    return (to_in(bb_re), to_in(bb_im), bcast(lam_re), bcast(lam_im), to_out(c_re), to_out(c_im))


def _ssm_kernel(u_ref, bre_ref, bim_ref, lre_ref, lim_ref, cre_ref, cim_ref, d_ref, wg_ref, bg_ref,
                o_ref, ubuf_ref, h0_ref, h1_ref, h2_ref, h3_ref, sre_ref, sim_ref,
                *, batch, steps, pitch):
    @pl.when(pl.program_id(0) == 0)
    def _():
        sre_ref[...] = jnp.zeros_like(sre_ref)
        sim_ref[...] = jnp.zeros_like(sim_ref)
        ubuf_ref[...] = jnp.zeros_like(ubuf_ref)

    for b in range(batch):
        ubuf_ref[b * pitch:b * pitch + steps, :] = u_ref[b]
    u = ubuf_ref[...]
    ub = u.astype(BF16)
    tiles = SSM_CHUNK_STATES // LANES
    ys = [None] * SSM_CHUNKS
    h_refs = (h0_ref, h1_ref, h2_ref, h3_ref)

    def input_matmul(c, part):
        w_ref = (bre_ref, bim_ref)[part]
        bu = jnp.dot(ub[:, c * LANES:(c + 1) * LANES], w_ref[c], preferred_element_type=F32)
        for i in range(tiles):
            h_refs[c][part, i] = bu[:, i * LANES:(i + 1) * LANES]

    def output_matmul(c):
        hs = [jnp.concatenate([h_refs[c][part, i] for i in range(tiles)], axis=1).astype(BF16)
              for part in range(2)]
        ys[c] = (jnp.dot(hs[0], cre_ref[c], preferred_element_type=F32)
                 - jnp.dot(hs[1], cim_ref[c], preferred_element_type=F32))

    def scan(chunks, fillers):
        slabs = [(c, i) for c in chunks for i in range(tiles)]
        lanes = [slice((c * tiles + i) * LANES, (c * tiles + i + 1) * LANES) for c, i in slabs]
        h_re = [sre_ref[:, ln] for ln in lanes]
        h_im = [sim_ref[:, ln] for ln in lanes]
        l_re = [lre_ref[:, ln] for ln in lanes]
        l_im = [lim_ref[:, ln] for ln in lanes]
        fillers = list(fillers)
        every = max(1, steps // max(1, len(fillers)))
        for t in range(steps):
            rows = pl.ds(t, batch, stride=pitch)
            for i, (c, tile) in enumerate(slabs):
                n_re = l_re[i] * h_re[i] - l_im[i] * h_im[i] + h_refs[c][0, tile, rows, :]
                n_im = l_re[i] * h_im[i] + l_im[i] * h_re[i] + h_refs[c][1, tile, rows, :]
                h_refs[c][0, tile, rows, :] = n_re
                h_refs[c][1, tile, rows, :] = n_im
                h_re[i], h_im[i] = n_re, n_im
            if fillers and (t + 1) % every == 0:
                fillers.pop(0)()
        for f in fillers:
            f()
        for i, ln in enumerate(lanes):
            sre_ref[:, ln] = h_re[i]
            sim_ref[:, ln] = h_im[i]

    first, second = (0, 1), (2, 3)
    for c in first:
        input_matmul(c, 0)
        input_matmul(c, 1)
    scan(first, [functools.partial(input_matmul, c, part) for c in second for part in range(2)])
    scan(second, [functools.partial(output_matmul, c) for c in first])
    for c in second:
        output_matmul(c)
    y = jnp.concatenate(ys, axis=-1) + d_ref[...] * u
    y = jax.nn.gelu(y)
    gate = jnp.dot(y.astype(BF16), wg_ref[...], preferred_element_type=F32) + bg_ref[...]
    out = y * jax.nn.sigmoid(gate)
    for b in range(batch):
        o_ref[b] = out[b * pitch:b * pitch + steps].astype(BF16)


def _ssm_mixer(u, mats, d_skip, w_glu, layer, b_glu, steps):
    batch, l, width = u.shape
    assert steps % 8 == 0
    pitch = steps + 4
    b_in_re, b_in_im, lam_re, lam_im, c_out_re, c_out_im = mats
    block = pl.BlockSpec((batch, steps, width), lambda i: (0, i, 0))
    return pl.pallas_call(
        functools.partial(_ssm_kernel, batch=batch, steps=steps, pitch=pitch),
        grid=(l // steps,),
        in_specs=[
            block,
            _const_spec(b_in_re.shape), _const_spec(b_in_im.shape),
            _const_spec(lam_re.shape), _const_spec(lam_im.shape),
            _const_spec(c_out_re.shape), _const_spec(c_out_im.shape),
            _const_spec((1, width)), _layer_spec(w_glu, layer), _const_spec((1, width)),
        ],
        out_specs=block,
        out_shape=jax.ShapeDtypeStruct((batch, l, width), BF16),
        scratch_shapes=[
            pltpu.VMEM((batch * pitch, width), F32),
            *[pltpu.VMEM((2, SSM_CHUNK_STATES // LANES, batch * pitch, LANES), F32)
              for _ in range(SSM_CHUNKS)],
            pltpu.VMEM((batch, SSM_STATES), F32), pltpu.VMEM((batch, SSM_STATES), F32),
        ],
        compiler_params=_params("arbitrary"),
        name="ssm_mixer",
    )(u, b_in_re, b_in_im, lam_re, lam_im, c_out_re, c_out_im, d_skip, w_glu, b_glu)


def _odd_in_kernel(x_ref, g_ref, w_ref, q_ref, k_ref, v_ref):
    subs = _row_subblocks(x_ref.shape[1])
    hns = [_rmsnorm(x_ref[0, rows, :], g_ref[...]).astype(BF16) for rows in subs]
    projs = [jnp.dot(hn, w_ref[...], preferred_element_type=F32) for hn in hns]
    for proj, rows in zip(projs, subs):
        q_ref[0, rows, :] = (proj[:, :SB_WIDTH] * (HEAD_DIM ** -0.5 * LOG2E)).astype(BF16)
        k_ref[0, rows, :] = proj[:, SB_WIDTH:2 * SB_WIDTH].astype(BF16)
        v_ref[0, rows, :] = proj[:, 2 * SB_WIDTH:].astype(BF16)


def _odd_in_proj(x, g, w, layer, tm):
    b, l, d = x.shape
    row = pl.BlockSpec((1, tm, SB_WIDTH), lambda i, j: (i, j, 0))
    out = jax.ShapeDtypeStruct((b, l, SB_WIDTH), BF16)
    return pl.pallas_call(
        _odd_in_kernel,
        grid=(b, l // tm),
        in_specs=[pl.BlockSpec((1, tm, d), lambda i, j: (i, j, 0)),
                  _const_spec((1, d)), _layer_spec(w, layer)],
        out_specs=[row, row, row],
        out_shape=[out, out, out],
        compiler_params=_params("parallel", "parallel"),
        name="odd_in_proj",
    )(x, g, w)


def _sb_streams(streams, tri):
    m = streams[0][0].shape[0]
    half = m // 2
    strict = (lax.broadcasted_iota(jnp.int32, (half, half), 1)
              < lax.broadcasted_iota(jnp.int32, (half, half), 0))

    def lower_triangle(fn, *blocks):
        quad = lambda r, c: fn(*[blk[r * half:(r + 1) * half, c * half:(c + 1) * half]
                                 for blk in blocks])
        top_left = jnp.where(strict, quad(0, 0), 0.0)
        bottom_right = jnp.where(strict, quad(1, 1), 0.0)
        top = jnp.concatenate([top_left, jnp.zeros_like(top_left)], axis=1)
        return jnp.concatenate([top, jnp.concatenate([quad(1, 0), bottom_right], axis=1)], axis=0)

    def softplus2(z):
        return jnp.maximum(z, jnp.log(1.0 + jnp.exp2(jnp.minimum(z, SB_EXP2_CLAMP))) * LOG2E)

    def scores(s):
        s["z"] = lax.dot_general(s["q"], s["k"], (((1,), (1,)), ((), ())),
                                 preferred_element_type=F32)

    def softplus(s):
        sp = lower_triangle(softplus2, s["z"]) if s["diag"] else softplus2(s["z"])
        s["sp"] = sp.astype(BF16)

    def suffix_sums(s):
        s["w"] = jnp.dot(s.pop("sp"), tri, preferred_element_type=F32)

    def weights(s):
        weight = lambda z, w: jnp.exp2(z - w)
        z = s.pop("z")
        p = lower_triangle(weight, z, s["w"]) if s["diag"] else weight(z, s["w"])
        s["p"] = p.astype(BF16)

    def values(s):
        s["pv"] = jnp.dot(s.pop("p"), s["v"], preferred_element_type=F32)

    stages = (scores, softplus, suffix_sums, weights, values)
    state = [dict(q=q, k=kh, v=vb, diag=diag) for q, kh, vb, diag in streams]
    groups = [state[i:i + SB_STREAM_GROUP] for i in range(0, len(state), SB_STREAM_GROUP)]
    for tick in range(len(groups) + len(stages) - 1):
        for g, members in enumerate(groups):
            if 0 <= tick - g < len(stages):
                for s in members:
                    stages[tick - g](s)
    return [(s["pv"], jnp.broadcast_to(s["w"][:, 0:1], (s["pv"].shape[0], LANES))) for s in state]


def _sb_kernel(q_ref, k_ref, v_ref, tri_ref, o_ref, c_ref, acc_ref, *, th, npair):
    qi = pl.program_id(2)
    lo_k = lax.broadcasted_iota(jnp.int32, (th, LANES), 1) < HEAD_DIM
    lo_q = lax.broadcasted_iota(jnp.int32, (th, LANES), 1) < HEAD_DIM
    tri = tri_ref[...]

    def load_kv(pair, j):
        rows = pl.ds(pl.multiple_of(j * th, th), th)
        lanes = slice(pair * LANES, (pair + 1) * LANES)
        kb = k_ref[0, rows, lanes]
        return (jnp.where(lo_k, kb, 0), jnp.where(lo_k, 0, kb)), v_ref[0, rows, lanes]

    def load_q(pair, half):
        return q_ref[0, half * th:(half + 1) * th, pair * LANES:(pair + 1) * LANES]

    j_diag = (2 * qi, 2 * qi + 1)
    has_left = (qi > 0).astype(F32)

    streams = []
    for pair in range(npair):
        kv_left = load_kv(pair, jnp.maximum(2 * qi - 1, 0))
        kv_diag = (load_kv(pair, j_diag[0]), load_kv(pair, j_diag[1]))
        for half in range(2):
            q = load_q(pair, half)
            (kd, vd) = kv_diag[half]
            (kl, vl) = kv_diag[0] if half == 1 else kv_left
            for h in range(2):
                streams += [(q, kd[h], vd, True), (q, kl[h], vl, False)]
    outs = iter(_sb_streams(streams, tri))
    for pair in range(npair):
        for half in range(2):
            for h in range(2):
                (pv_d, c_d), (pv_l, c_l) = next(outs), next(outs)
                scale = jnp.exp2(-c_d)
                if half == 0:
                    scale, c_l = scale * has_left, c_l * has_left
                acc_ref[pair, half, h] = pv_d + pv_l * scale
                c_ref[pair, half, h] = c_d + c_l

    def visit_rest(pair, half):
        q = load_q(pair, half)
        n_rest = jnp.maximum(j_diag[half] - 1, 0)

        def cond(carry):
            i, c_min = carry
            return jnp.logical_and(i < n_rest, c_min < SB_DEAD_LOG2)

        def body(carry):
            i, _ = carry
            ks, vb = load_kv(pair, n_rest - 1 - i)
            outs = _sb_streams([(q, ks[0], vb, False), (q, ks[1], vb, False)], tri)
            for h, (pv, c_blk) in enumerate(outs):
                c = c_ref[pair, half, h]
                acc_ref[pair, half, h] += pv * jnp.exp2(-c)
                c_ref[pair, half, h] = c + c_blk
            return i + 1, jnp.min(c_ref[pair, half])

        lax.while_loop(cond, body, (jnp.int32(0), jnp.min(c_ref[pair, half])))

    @pl.when(jnp.min(c_ref[...]) < SB_DEAD_LOG2)
    def _():
        for pair in range(npair):
            for half in range(2):
                visit_rest(pair, half)

    for pair in range(npair):
        for half in range(2):
            o_ref[0, half * th:(half + 1) * th, pair * LANES:(pair + 1) * LANES] = jnp.where(
                lo_q, acc_ref[pair, half, 0], acc_ref[pair, half, 1]).astype(BF16)


def _sb_attention(q, k, v, th, npair):
    b, l, width = q.shape
    tri = (jnp.arange(th)[:, None] >= jnp.arange(th)[None, :]).astype(BF16)
    tq = 2 * th
    lanes = npair * LANES
    state = pltpu.VMEM((npair, 2, 2, th, LANES), F32)
    return pl.pallas_call(
        functools.partial(_sb_kernel, th=th, npair=npair),
        grid=(b, width // lanes, l // tq),
        in_specs=[
            pl.BlockSpec((1, tq, lanes), lambda i, h, j: (i, j, h)),
            pl.BlockSpec((1, l, lanes), lambda i, h, j: (i, 0, h)),
            pl.BlockSpec((1, l, lanes), lambda i, h, j: (i, 0, h)),
            _const_spec((th, th)),
        ],
        out_specs=pl.BlockSpec((1, tq, lanes), lambda i, h, j: (i, j, h)),
        out_shape=jax.ShapeDtypeStruct((b, l, width), BF16),
        scratch_shapes=[state, state],
        compiler_params=_params("parallel", "parallel", "arbitrary"),
        name="sb_attention",
    )(q, k, v, tri)


def _post_kernel(*refs, n_mix):
    x_ref = refs[0]
    mix_refs = refs[1:1 + n_mix]
    wo_ref, g_ref, wg_ref, wu_ref, wd_ref, o_ref = refs[1 + n_mix:]
    subs = _row_subblocks(x_ref.shape[1])
    mixed = [jnp.concatenate([m_ref[0, rows, :] for m_ref in mix_refs], axis=-1) for rows in subs]
    xs = [x_ref[0, rows, :] + jnp.dot(m, wo_ref[...], preferred_element_type=F32)
          for m, rows in zip(mixed, subs)]
    hns = [_rmsnorm(x, g_ref[...]).astype(BF16) for x in xs]
    gates = [jnp.dot(hn, wg_ref[...], preferred_element_type=F32) for hn in hns]
    ups = [jnp.dot(hn, wu_ref[...], preferred_element_type=F32) for hn in hns]
    acts = [(gate * jax.nn.sigmoid(gate) * up).astype(BF16) for gate, up in zip(gates, ups)]
    for x, act, rows in zip(xs, acts, subs):
        o_ref[0, rows, :] = x + jnp.dot(act, wd_ref[...], preferred_element_type=F32)


def _post(x, mixes, w_out, mixer_layer, g, w_gate, w_up, w_down, layer, tm):
    b, l, d = x.shape
    xspec = pl.BlockSpec((1, tm, d), lambda i, j: (i, j, 0))
    mix_specs = [pl.BlockSpec((1, tm, m.shape[-1]), lambda i, j: (i, j, 0)) for m in mixes]
    return pl.pallas_call(
        functools.partial(_post_kernel, n_mix=len(mixes)),
        grid=(b, l // tm),
        in_specs=[xspec, *mix_specs, _layer_spec(w_out, mixer_layer),
                  _const_spec((1, d)), _layer_spec(w_gate, layer), _layer_spec(w_up, layer),
                  _layer_spec(w_down, layer)],
        out_specs=xspec,
        out_shape=jax.ShapeDtypeStruct(x.shape, F32),
        compiler_params=_params("parallel", "parallel"),
        name="out_proj_ffn",
    )(x, *mixes, w_out, g, w_gate, w_up, w_down)


def _tiles(l):
    tm = min(512, l)
    tm_in = min(1024, l)
    th = min(256, l // 2)
    steps = min(128, l)
    nsub = min(8, l // SWA_BLOCK)
    return tm, tm_in, th, steps, nsub


def kernel(x, even_norm, even_w_in, q_norm, k_norm, sinks, ssm_a_re, ssm_a_im, ssm_b_re, ssm_b_im,
           ssm_c_re, ssm_c_im, ssm_d, ssm_log_dt, ssm_w_glu, ssm_b_glu, even_w_out, odd_norm,
           odd_w_in, odd_w_out, ffn_norm, ffn_w_gate, ffn_w_up, ffn_w_down):
    b, l, d = x.shape
    assert d == D_MODEL and l % SWA_BLOCK == 0
    tm, tm_in, th, steps, nsub = _tiles(l)
    depth = ffn_norm.shape[0]
    row = lambda t: t.reshape(1, -1)
    (even_w_in, ssm_w_glu, even_w_out, odd_w_in, odd_w_out, ffn_w_gate, ffn_w_up, ffn_w_down) = (
        t.astype(BF16) for t in (even_w_in, ssm_w_glu, even_w_out, odd_w_in, odd_w_out,
                                 ffn_w_gate, ffn_w_up, ffn_w_down))
    for layer in range(depth):
        i = layer // 2
        if layer % 2 == 0:
            q, k, v, u = _even_in_proj(
                x, row(even_norm[i]), even_w_in, i,
                row(jnp.tile(q_norm[i], 2)), row(jnp.tile(k_norm[i], 2)), tm_in)
            o_attn = _swa_attention(sinks[i], q, k, v, nsub)
            mats = _ssm_prepare(ssm_a_re[i], ssm_a_im[i], ssm_b_re[i], ssm_b_im[i],
                                ssm_c_re[i], ssm_c_im[i], ssm_log_dt[i], b)
            o_ssm = _ssm_mixer(u, mats, row(ssm_d[i]), ssm_w_glu, i, row(ssm_b_glu[i]), steps)
            mixes, w_out = [o_attn, o_ssm], even_w_out
        else:
            q, k, v = _odd_in_proj(x, row(odd_norm[i]), odd_w_in, i, tm_in)
            mixes, w_out = [_sb_attention(q, k, v, th, SB_PAIRS_PER_STEP)], odd_w_out
        x = _post(x, mixes, w_out, i, row(ffn_norm[layer]), ffn_w_gate, ffn_w_up, ffn_w_down,
                  layer, tm)
    return x
```

```python
import functools
import math

import jax
import jax.numpy as jnp
from jax import lax
from jax.experimental import pallas as pl
from jax.experimental.pallas import tpu as pltpu

F32 = jnp.float32
BF16 = jnp.bfloat16

D_MODEL = 1024
HEAD_DIM = 64
EPS = 1e-6
LANES = 128

SWA_Q_HEADS = 8
SWA_KV_HEADS = 2
SWA_BLOCK = 128
SWA_WIDTH = SWA_Q_HEADS * HEAD_DIM
KV_WIDTH = SWA_KV_HEADS * HEAD_DIM

SSM_WIDTH = D_MODEL // 2
SSM_GROUP_CH = 16
SSM_GROUPS = SSM_WIDTH // SSM_GROUP_CH
SSM_STATE = 64
SSM_STATES = SSM_GROUPS * SSM_STATE
SSM_CHUNKS = SSM_WIDTH // LANES
SSM_CHUNK_STATES = SSM_STATES // SSM_CHUNKS
EVEN_IN = SWA_WIDTH + 2 * KV_WIDTH + SSM_WIDTH

SB_HEADS = D_MODEL // HEAD_DIM
SB_WIDTH = SB_HEADS * HEAD_DIM
LOG2E = math.log2(math.e)
SB_DEAD_LOG2 = 1e30
SB_EXP2_CLAMP = 126.0
SB_PAIRS_PER_STEP = 4
SB_STREAM_GROUP = 2

VMEM_LIMIT_BYTES = 56 * 1024 * 1024
ROW_SPLIT = 2


def _params(*sem):
    return pltpu.CompilerParams(dimension_semantics=sem, vmem_limit_bytes=VMEM_LIMIT_BYTES)


def _const_spec(shape):
    nd = len(shape)
    return pl.BlockSpec(shape, lambda *_: (0,) * nd, pipeline_mode=pl.Buffered(1))


def _layer_spec(stacked, layer):
    rest = stacked.shape[1:]
    return pl.BlockSpec((None, *rest), lambda *_: (layer,) + (0,) * len(rest),
                        pipeline_mode=pl.Buffered(1))


def _row_subblocks(tm):
    step = tm // ROW_SPLIT
    return [slice(i * step, (i + 1) * step) for i in range(ROW_SPLIT)]


def _rmsnorm(x, g):
    ms = jnp.mean(x * x, axis=-1, keepdims=True)
    return x * lax.rsqrt(ms + EPS) * g


def _pair_rmsnorm(x, g2):
    lo = lax.broadcasted_iota(jnp.int32, x.shape, 1) < HEAD_DIM
    sq = x * x
    s_lo = jnp.sum(jnp.where(lo, sq, 0.0), axis=-1, keepdims=True)
    s_hi = jnp.sum(jnp.where(lo, 0.0, sq), axis=-1, keepdims=True)
    ms = jnp.where(lo, s_lo, s_hi) * (1.0 / HEAD_DIM)
    return x * lax.rsqrt(ms + EPS) * g2


def _even_in_kernel(x_ref, g_ref, w_ref, qg_ref, kg_ref, q_ref, k_ref, v_ref, u_ref):
    subs = _row_subblocks(x_ref.shape[1])
    hns = [_rmsnorm(x_ref[0, rows, :], g_ref[...]).astype(BF16) for rows in subs]
    projs = [jnp.dot(hn, w_ref[...], preferred_element_type=F32) for hn in hns]
    scale = HEAD_DIM ** -0.5 * LOG2E
    for proj, rows in zip(projs, subs):
        for p in range(SWA_WIDTH // LANES):
            qp = _pair_rmsnorm(proj[:, p * LANES:(p + 1) * LANES], qg_ref[...])
            q_ref[0, rows, p * LANES:(p + 1) * LANES] = (qp * scale).astype(BF16)
        k = _pair_rmsnorm(proj[:, SWA_WIDTH:SWA_WIDTH + KV_WIDTH], kg_ref[...])
        k_ref[0, rows, :] = k.astype(BF16)
        v_ref[0, rows, :] = proj[:, SWA_WIDTH + KV_WIDTH:SWA_WIDTH + 2 * KV_WIDTH].astype(BF16)
        u_ref[0, rows, :] = proj[:, SWA_WIDTH + 2 * KV_WIDTH:]


def _even_in_proj(x, g, w, layer, qg, kg, tm):
    b, l, d = x.shape
    return pl.pallas_call(
        _even_in_kernel,
        grid=(b, l // tm),
        in_specs=[
            pl.BlockSpec((1, tm, d), lambda i, j: (i, j, 0)),
            _const_spec((1, d)),
            _layer_spec(w, layer),
            _const_spec((1, LANES)),
            _const_spec((1, LANES)),
        ],
        out_specs=[
            pl.BlockSpec((1, tm, SWA_WIDTH), lambda i, j: (i, j, 0)),
            pl.BlockSpec((1, tm, KV_WIDTH), lambda i, j: (i, j, 0)),
            pl.BlockSpec((1, tm, KV_WIDTH), lambda i, j: (i, j, 0)),
            pl.BlockSpec((1, tm, SSM_WIDTH), lambda i, j: (i, j, 0)),
        ],
        out_shape=[
            jax.ShapeDtypeStruct((b, l, SWA_WIDTH), BF16),
            jax.ShapeDtypeStruct((b, l, KV_WIDTH), BF16),
            jax.ShapeDtypeStruct((b, l, KV_WIDTH), BF16),
            jax.ShapeDtypeStruct((b, l, SSM_WIDTH), F32),
        ],
        compiler_params=_params("parallel", "parallel"),
        name="even_in_proj",
    )(x, g, w, qg, kg)


def _swa_kernel(sink_ref, q_ref, kc_ref, kp_ref, vc_ref, vp_ref, o_ref, *, nsub):
    n = pl.program_id(1)
    blk = SWA_BLOCK
    kall = jnp.concatenate([kp_ref[0], kc_ref[0]], axis=0).astype(F32)
    vall = jnp.concatenate([vp_ref[0], vc_ref[0]], axis=0).astype(F32)
    lo = lax.broadcasted_iota(jnp.int32, kall.shape, 1) < HEAD_DIM

    def halves(t):
        g0_lo = jnp.where(lo, t, 0.0)
        g1_hi = jnp.where(lo, 0.0, t)
        g0_hi = pltpu.roll(g0_lo, HEAD_DIM, 1)
        g1_lo = pltpu.roll(g1_hi, HEAD_DIM, 1)
        return [[g0_lo.astype(BF16), g0_hi.astype(BF16)], [g1_lo.astype(BF16), g1_hi.astype(BF16)]]

    ks, vs = halves(kall), halves(vall)
    qq = lax.broadcasted_iota(jnp.int32, (blk, 2 * blk), 0)
    kk = lax.broadcasted_iota(jnp.int32, (blk, 2 * blk), 1)
    diff = qq + blk - kk
    band = (diff >= 0) & (diff < blk)
    first_key = jnp.where(n > 0, 0, blk)
    band_first = band & (kk >= first_key)

    streams = []
    for s in range(nsub):
        keys = slice(s * blk, (s + 2) * blk)
        for p in range(SWA_WIDTH // LANES):
            g = (2 * p) // (SWA_Q_HEADS // SWA_KV_HEADS)
            for e in range(2):
                streams.append(dict(
                    q=q_ref[0, s * blk:(s + 1) * blk, p * LANES:(p + 1) * LANES],
                    k=ks[g][e][keys], v=vs[g][e][keys],
                    sink=sink_ref[2 * p + e] * LOG2E, mask=band_first if s == 0 else band))
    group = len(streams) // nsub
    for g0 in range(0, len(streams), group):
        for st in streams[g0:g0 + group]:
            st["z"] = lax.dot_general(st["q"], st["k"], (((1,), (1,)), ((), ())),
                                      preferred_element_type=F32)
        for st in streams[g0:g0 + group]:
            z = jnp.where(st["mask"], st.pop("z"), -jnp.inf)
            m = jnp.maximum(jnp.max(z, axis=-1, keepdims=True), st["sink"])
            pe = jnp.exp2(z - m)
            st["den"] = jnp.sum(pe, axis=-1, keepdims=True) + jnp.exp2(st["sink"] - m)
            st["pe"] = pe.astype(BF16)
        for st in streams[g0:g0 + group]:
            st["o"] = jnp.dot(st.pop("pe"), st["v"], preferred_element_type=F32) / st["den"]
    for s in range(nsub):
        for p in range(SWA_WIDTH // LANES):
            i = (s * (SWA_WIDTH // LANES) + p) * 2
            o_ref[0, s * blk:(s + 1) * blk, p * LANES:(p + 1) * LANES] = (
                streams[i]["o"] + streams[i + 1]["o"]).astype(BF16)


def _swa_attention(sinks, q, k, v, nsub):
    b, l, _ = q.shape
    blk = SWA_BLOCK * nsub
    cur = lambda i, j: (i, j, 0)
    prev = lambda i, j: (i, jnp.maximum(j * nsub - 1, 0), 0)
    return pl.pallas_call(
        functools.partial(_swa_kernel, nsub=nsub),
        grid=(b, l // blk),
        in_specs=[
            pl.BlockSpec(memory_space=pltpu.SMEM),
            pl.BlockSpec((1, blk, SWA_WIDTH), cur),
            pl.BlockSpec((1, blk, KV_WIDTH), cur),
            pl.BlockSpec((1, SWA_BLOCK, KV_WIDTH), prev),
            pl.BlockSpec((1, blk, KV_WIDTH), cur),
            pl.BlockSpec((1, SWA_BLOCK, KV_WIDTH), prev),
        ],
        out_specs=pl.BlockSpec((1, blk, SWA_WIDTH), cur),
        out_shape=jax.ShapeDtypeStruct((b, l, SWA_WIDTH), BF16),
        compiler_params=_params("parallel", "parallel"),
        name="swa_attention",
    )(sinks, q, k, k, v, v)


def _ssm_disc_kernel(are_ref, aim_ref, ldt_ref, lre_ref, lim_ref, wre_ref, wim_ref):
    a_re, a_im = are_ref[...], aim_ref[...]
    dt = jnp.exp(ldt_ref[...])
    mag = jnp.exp(a_re * dt)
    lam_re = mag * jnp.cos(a_im * dt)
    lam_im = mag * jnp.sin(a_im * dt)
    den = a_re * a_re + a_im * a_im
    lre_ref[...] = lam_re
    lim_ref[...] = lam_im
    wre_ref[...] = ((lam_re - 1.0) * a_re + lam_im * a_im) / den
    wim_ref[...] = (lam_im * a_re - (lam_re - 1.0) * a_im) / den


def _ssm_bbar_kernel(wre_ref, wim_ref, bre_ref, bim_ref, ore_ref, oim_ref):
    w_re, w_im, b_re, b_im = wre_ref[...], wim_ref[...], bre_ref[...], bim_ref[...]
    ore_ref[...] = w_re * b_re - w_im * b_im
    oim_ref[...] = w_re * b_im + w_im * b_re


def _block_diag(t):
    ch, gl, r, c = t.shape
    eye = jnp.eye(gl, dtype=t.dtype)
    return (t[:, :, :, None, :] * eye[None, :, None, :, None]).reshape(ch, gl * r, gl * c)


def _ssm_prepare(a_re, a_im, b_re, b_im, c_re, c_im, log_dt, batch):
    g, n, p = SSM_GROUPS, SSM_STATE, SSM_GROUP_CH
    gn = jax.ShapeDtypeStruct((g, n), F32)
    ldt = jnp.broadcast_to(log_dt[:, None], (g, n))
    lam_re, lam_im, w_re, w_im = pl.pallas_call(
        _ssm_disc_kernel, out_shape=[gn, gn, gn, gn], name="ssm_discretise")(a_re, a_im, ldt)
    gnp = jax.ShapeDtypeStruct((g, n * p), F32)
    bb_re, bb_im = pl.pallas_call(_ssm_bbar_kernel, out_shape=[gnp, gnp], name="ssm_bbar")(
        jnp.repeat(w_re, p, axis=1), jnp.repeat(w_im, p, axis=1),
        b_re.reshape(g, n * p), b_im.reshape(g, n * p))
    gl = g // SSM_CHUNKS
    to_in = lambda t: _block_diag(
        t.reshape(SSM_CHUNKS, gl, n, p).transpose(0, 1, 3, 2)).astype(BF16)
    to_out = lambda t: _block_diag(
        t.reshape(SSM_CHUNKS, gl, p, n).transpose(0, 1, 3, 2)).astype(BF16)
    bcast = lambda t: jnp.broadcast_to(t.reshape(1, SSM_STATES), (batch, SSM_STATES))
    return (to_in(bb_re), to_in(bb_im), bcast(lam_re), bcast(lam_im), to_out(c_re), to_out(c_im))


def _ssm_kernel(u_ref, bre_ref, bim_ref, lre_ref, lim_ref, cre_ref, cim_ref, d_ref, wg_ref, bg_ref,
                o_ref, ubuf_ref, h0_ref, h1_ref, h2_ref, h3_ref, sre_ref, sim_ref,
                *, batch, steps, pitch):
    @pl.when(pl.program_id(0) == 0)
    def _():
        sre_ref[...] = jnp.zeros_like(sre_ref)
        sim_ref[...] = jnp.zeros_like(sim_ref)
        ubuf_ref[...] = jnp.zeros_like(ubuf_ref)

    for b in range(batch):
        ubuf_ref[b * pitch:b * pitch + steps, :] = u_ref[b]
    u = ubuf_ref[...]
    ub = u.astype(BF16)
    tiles = SSM_CHUNK_STATES // LANES
    ys = [None] * SSM_CHUNKS
    h_refs = (h0_ref, h1_ref, h2_ref, h3_ref)

    def input_matmul(c, part):
        w_ref = (bre_ref, bim_ref)[part]
        bu = jnp.dot(ub[:, c * LANES:(c + 1) * LANES], w_ref[c], preferred_element_type=F32)
        for i in range(tiles):
            h_refs[c][part, i] = bu[:, i * LANES:(i + 1) * LANES]

    def output_matmul(c):
        hs = [jnp.concatenate([h_refs[c][part, i] for i in range(tiles)], axis=1).astype(BF16)
              for part in range(2)]
        ys[c] = (jnp.dot(hs[0], cre_ref[c], preferred_element_type=F32)
                 - jnp.dot(hs[1], cim_ref[c], preferred_element_type=F32))

    def scan(chunks, fillers):
        slabs = [(c, i) for c in chunks for i in range(tiles)]
        lanes = [slice((c * tiles + i) * LANES, (c * tiles + i + 1) * LANES) for c, i in slabs]
        h_re = [sre_ref[:, ln] for ln in lanes]
        h_im = [sim_ref[:, ln] for ln in lanes]
        l_re = [lre_ref[:, ln] for ln in lanes]
        l_im = [lim_ref[:, ln] for ln in lanes]
        fillers = list(fillers)
        every = max(1, steps // max(1, len(fillers)))
        for t in range(steps):
            rows = pl.ds(t, batch, stride=pitch)
            for i, (c, tile) in enumerate(slabs):
                n_re = l_re[i] * h_re[i] - l_im[i] * h_im[i] + h_refs[c][0, tile, rows, :]
                n_im = l_re[i] * h_im[i] + l_im[i] * h_re[i] + h_refs[c][1, tile, rows, :]
                h_refs[c][0, tile, rows, :] = n_re
                h_refs[c][1, tile, rows, :] = n_im
                h_re[i], h_im[i] = n_re, n_im
            if fillers and (t + 1) % every == 0:
                fillers.pop(0)()
        for f in fillers:
            f()
        for i, ln in enumerate(lanes):
            sre_ref[:, ln] = h_re[i]
            sim_ref[:, ln] = h_im[i]

    first, second = (0, 1), (2, 3)
    for c in first:
        input_matmul(c, 0)
        input_matmul(c, 1)
    scan(first, [functools.partial(input_matmul, c, part) for c in second for part in range(2)])
    scan(second, [functools.partial(output_matmul, c) for c in first])
    for c in second:
        output_matmul(c)
    y = jnp.concatenate(ys, axis=-1) + d_ref[...] * u
    y = jax.nn.gelu(y)
    gate = jnp.dot(y.astype(BF16), wg_ref[...], preferred_element_type=F32) + bg_ref[...]
    out = y * jax.nn.sigmoid(gate)
    for b in range(batch):
        o_ref[b] = out[b * pitch:b * pitch + steps].astype(BF16)


def _ssm_mixer(u, mats, d_skip, w_glu, layer, b_glu, steps):
    batch, l, width = u.shape
    assert steps % 8 == 0
    pitch = steps + 4
    b_in_re, b_in_im, lam_re, lam_im, c_out_re, c_out_im = mats
    block = pl.BlockSpec((batch, steps, width), lambda i: (0, i, 0))
    return pl.pallas_call(
        functools.partial(_ssm_kernel, batch=batch, steps=steps, pitch=pitch),
        grid=(l // steps,),
        in_specs=[
            block,
            _const_spec(b_in_re.shape), _const_spec(b_in_im.shape),
            _const_spec(lam_re.shape), _const_spec(lam_im.shape),
            _const_spec(c_out_re.shape), _const_spec(c_out_im.shape),
            _const_spec((1, width)), _layer_spec(w_glu, layer), _const_spec((1, width)),
        ],
        out_specs=block,
        out_shape=jax.ShapeDtypeStruct((batch, l, width), BF16),
        scratch_shapes=[
            pltpu.VMEM((batch * pitch, width), F32),
            *[pltpu.VMEM((2, SSM_CHUNK_STATES // LANES, batch * pitch, LANES), F32)
              for _ in range(SSM_CHUNKS)],
            pltpu.VMEM((batch, SSM_STATES), F32), pltpu.VMEM((batch, SSM_STATES), F32),
        ],
        compiler_params=_params("arbitrary"),
        name="ssm_mixer",
    )(u, b_in_re, b_in_im, lam_re, lam_im, c_out_re, c_out_im, d_skip, w_glu, b_glu)


def _odd_in_kernel(x_ref, g_ref, w_ref, q_ref, k_ref, v_ref):
    subs = _row_subblocks(x_ref.shape[1])
    hns = [_rmsnorm(x_ref[0, rows, :], g_ref[...]).astype(BF16) for rows in subs]
    projs = [jnp.dot(hn, w_ref[...], preferred_element_type=F32) for hn in hns]
    for proj, rows in zip(projs, subs):
        q_ref[0, rows, :] = (proj[:, :SB_WIDTH] * (HEAD_DIM ** -0.5 * LOG2E)).astype(BF16)
        k_ref[0, rows, :] = proj[:, SB_WIDTH:2 * SB_WIDTH].astype(BF16)
        v_ref[0, rows, :] = proj[:, 2 * SB_WIDTH:].astype(BF16)


def _odd_in_proj(x, g, w, layer, tm):
    b, l, d = x.shape
    row = pl.BlockSpec((1, tm, SB_WIDTH), lambda i, j: (i, j, 0))
    out = jax.ShapeDtypeStruct((b, l, SB_WIDTH), BF16)
    return pl.pallas_call(
        _odd_in_kernel,
        grid=(b, l // tm),
        in_specs=[pl.BlockSpec((1, tm, d), lambda i, j: (i, j, 0)),
                  _const_spec((1, d)), _layer_spec(w, layer)],
        out_specs=[row, row, row],
        out_shape=[out, out, out],
        compiler_params=_params("parallel", "parallel"),
        name="odd_in_proj",
    )(x, g, w)


def _sb_streams(streams, tri):
    m = streams[0][0].shape[0]
    half = m // 2
    strict = (lax.broadcasted_iota(jnp.int32, (half, half), 1)
              < lax.broadcasted_iota(jnp.int32, (half, half), 0))

    def lower_triangle(fn, *blocks):
        quad = lambda r, c: fn(*[blk[r * half:(r + 1) * half, c * half:(c + 1) * half]
                                 for blk in blocks])
        top_left = jnp.where(strict, quad(0, 0), 0.0)
        bottom_right = jnp.where(strict, quad(1, 1), 0.0)
        top = jnp.concatenate([top_left, jnp.zeros_like(top_left)], axis=1)
        return jnp.concatenate([top, jnp.concatenate([quad(1, 0), bottom_right], axis=1)], axis=0)

    def softplus2(z):
        return jnp.maximum(z, jnp.log(1.0 + jnp.exp2(jnp.minimum(z, SB_EXP2_CLAMP))) * LOG2E)

    def scores(s):
        s["z"] = lax.dot_general(s["q"], s["k"], (((1,), (1,)), ((), ())),
                                 preferred_element_type=F32)

    def softplus(s):
        sp = lower_triangle(softplus2, s["z"]) if s["diag"] else softplus2(s["z"])
        s["sp"] = sp.astype(BF16)

    def suffix_sums(s):
        s["w"] = jnp.dot(s.pop("sp"), tri, preferred_element_type=F32)

    def weights(s):
        weight = lambda z, w: jnp.exp2(z - w)
        z = s.pop("z")
        p = lower_triangle(weight, z, s["w"]) if s["diag"] else weight(z, s["w"])
        s["p"] = p.astype(BF16)

    def values(s):
        s["pv"] = jnp.dot(s.pop("p"), s["v"], preferred_element_type=F32)

    stages = (scores, softplus, suffix_sums, weights, values)
    state = [dict(q=q, k=kh, v=vb, diag=diag) for q, kh, vb, diag in streams]
    groups = [state[i:i + SB_STREAM_GROUP] for i in range(0, len(state), SB_STREAM_GROUP)]
    for tick in range(len(groups) + len(stages) - 1):
        for g, members in enumerate(groups):
            if 0 <= tick - g < len(stages):
                for s in members:
                    stages[tick - g](s)
    return [(s["pv"], jnp.broadcast_to(s["w"][:, 0:1], (s["pv"].shape[0], LANES))) for s in state]


def _sb_kernel(q_ref, k_ref, v_ref, tri_ref, o_ref, c_ref, acc_ref, *, th, npair):
    qi = pl.program_id(2)
    lo_k = lax.broadcasted_iota(jnp.int32, (th, LANES), 1) < HEAD_DIM
    lo_q = lax.broadcasted_iota(jnp.int32, (th, LANES), 1) < HEAD_DIM
    tri = tri_ref[...]

    def load_kv(pair, j):
        rows = pl.ds(pl.multiple_of(j * th, th), th)
        lanes = slice(pair * LANES, (pair + 1) * LANES)
        kb = k_ref[0, rows, lanes]
        return (jnp.where(lo_k, kb, 0), jnp.where(lo_k, 0, kb)), v_ref[0, rows, lanes]

    def load_q(pair, half):
        return q_ref[0, half * th:(half + 1) * th, pair * LANES:(pair + 1) * LANES]

    j_diag = (2 * qi, 2 * qi + 1)
    has_left = (qi > 0).astype(F32)

    streams = []
    for pair in range(npair):
        kv_left = load_kv(pair, jnp.maximum(2 * qi - 1, 0))
        kv_diag = (load_kv(pair, j_diag[0]), load_kv(pair, j_diag[1]))
        for half in range(2):
            q = load_q(pair, half)
            (kd, vd) = kv_diag[half]
            (kl, vl) = kv_diag[0] if half == 1 else kv_left
            for h in range(2):
                streams += [(q, kd[h], vd, True), (q, kl[h], vl, False)]
    outs = iter(_sb_streams(streams, tri))
    for pair in range(npair):
        for half in range(2):
            for h in range(2):
                (pv_d, c_d), (pv_l, c_l) = next(outs), next(outs)
                scale = jnp.exp2(-c_d)
                if half == 0:
                    scale, c_l = scale * has_left, c_l * has_left
                acc_ref[pair, half, h] = pv_d + pv_l * scale
                c_ref[pair, half, h] = c_d + c_l

    def visit_rest(pair, half):
        q = load_q(pair, half)
        n_rest = jnp.maximum(j_diag[half] - 1, 0)

        def cond(carry):
            i, c_min = carry
            return jnp.logical_and(i < n_rest, c_min < SB_DEAD_LOG2)

        def body(carry):
            i, _ = carry
            ks, vb = load_kv(pair, n_rest - 1 - i)
            outs = _sb_streams([(q, ks[0], vb, False), (q, ks[1], vb, False)], tri)
            for h, (pv, c_blk) in enumerate(outs):
                c = c_ref[pair, half, h]
                acc_ref[pair, half, h] += pv * jnp.exp2(-c)
                c_ref[pair, half, h] = c + c_blk
            return i + 1, jnp.min(c_ref[pair, half])

        lax.while_loop(cond, body, (jnp.int32(0), jnp.min(c_ref[pair, half])))

    @pl.when(jnp.min(c_ref[...]) < SB_DEAD_LOG2)
    def _():
        for pair in range(npair):
            for half in range(2):
                visit_rest(pair, half)

    for pair in range(npair):
        for half in range(2):
            o_ref[0, half * th:(half + 1) * th, pair * LANES:(pair + 1) * LANES] = jnp.where(
                lo_q, acc_ref[pair, half, 0], acc_ref[pair, half, 1]).astype(BF16)


def _sb_attention(q, k, v, th, npair):
    b, l, width = q.shape
    tri = (jnp.arange(th)[:, None] >= jnp.arange(th)[None, :]).astype(BF16)
    tq = 2 * th
    lanes = npair * LANES
    state = pltpu.VMEM((npair, 2, 2, th, LANES), F32)
    return pl.pallas_call(
        functools.partial(_sb_kernel, th=th, npair=npair),
        grid=(b, width // lanes, l // tq),
        in_specs=[
            pl.BlockSpec((1, tq, lanes), lambda i, h, j: (i, j, h)),
            pl.BlockSpec((1, l, lanes), lambda i, h, j: (i, 0, h)),
            pl.BlockSpec((1, l, lanes), lambda i, h, j: (i, 0, h)),
            _const_spec((th, th)),
        ],
        out_specs=pl.BlockSpec((1, tq, lanes), lambda i, h, j: (i, j, h)),
        out_shape=jax.ShapeDtypeStruct((b, l, width), BF16),
        scratch_shapes=[state, state],
        compiler_params=_params("parallel", "parallel", "arbitrary"),
        name="sb_attention",
    )(q, k, v, tri)


def _post_kernel(*refs, n_mix):
    x_ref = refs[0]
    mix_refs = refs[1:1 + n_mix]
    wo_ref, g_ref, wg_ref, wu_ref, wd_ref, o_ref = refs[1 + n_mix:]
    subs = _row_subblocks(x_ref.shape[1])
    mixed = [jnp.concatenate([m_ref[0, rows, :] for m_ref in mix_refs], axis=-1) for rows in subs]
    xs = [x_ref[0, rows, :] + jnp.dot(m, wo_ref[...], preferred_element_type=F32)
          for m, rows in zip(mixed, subs)]
    hns = [_rmsnorm(x, g_ref[...]).astype(BF16) for x in xs]
    gates = [jnp.dot(hn, wg_ref[...], preferred_element_type=F32) for hn in hns]
    ups = [jnp.dot(hn, wu_ref[...], preferred_element_type=F32) for hn in hns]
    acts = [(gate * jax.nn.sigmoid(gate) * up).astype(BF16) for gate, up in zip(gates, ups)]
    for x, act, rows in zip(xs, acts, subs):
        o_ref[0, rows, :] = x + jnp.dot(act, wd_ref[...], preferred_element_type=F32)


def _post(x, mixes, w_out, mixer_layer, g, w_gate, w_up, w_down, layer, tm):
    b, l, d = x.shape
    xspec = pl.BlockSpec((1, tm, d), lambda i, j: (i, j, 0))
    mix_specs = [pl.BlockSpec((1, tm, m.shape[-1]), lambda i, j: (i, j, 0)) for m in mixes]
    return pl.pallas_call(
        functools.partial(_post_kernel, n_mix=len(mixes)),
        grid=(b, l // tm),
        in_specs=[xspec, *mix_specs, _layer_spec(w_out, mixer_layer),
                  _const_spec((1, d)), _layer_spec(w_gate, layer), _layer_spec(w_up, layer),
                  _layer_spec(w_down, layer)],
        out_specs=xspec,
        out_shape=jax.ShapeDtypeStruct(x.shape, F32),
        compiler_params=_params("parallel", "parallel"),
        name="out_proj_ffn",
    )(x, *mixes, w_out, g, w_gate, w_up, w_down)


def _tiles(l):
    tm = min(512, l)
    tm_in = min(1024, l)
    th = min(256, l // 2)
    steps = min(128, l)
    nsub = min(8, l // SWA_BLOCK)
    return tm, tm_in, th, steps, nsub


def kernel(x, even_norm, even_w_in, q_norm, k_norm, sinks, ssm_a_re, ssm_a_im, ssm_b_re, ssm_b_im,
           ssm_c_re, ssm_c_im, ssm_d, ssm_log_dt, ssm_w_glu, ssm_b_glu, even_w_out, odd_norm,
           odd_w_in, odd_w_out, ffn_norm, ffn_w_gate, ffn_w_up, ffn_w_down):
    b, l, d = x.shape
    assert d == D_MODEL and l % SWA_BLOCK == 0
    tm, tm_in, th, steps, nsub = _tiles(l)
    depth = ffn_norm.shape[0]
    row = lambda t: t.reshape(1, -1)
    (even_w_in, ssm_w_glu, even_w_out, odd_w_in, odd_w_out, ffn_w_gate, ffn_w_up, ffn_w_down) = (
        t.astype(BF16) for t in (even_w_in, ssm_w_glu, even_w_out, odd_w_in, odd_w_out,
                                 ffn_w_gate, ffn_w_up, ffn_w_down))
    for layer in range(depth):
        i = layer // 2
        if layer % 2 == 0:
            q, k, v, u = _even_in_proj(
                x, row(even_norm[i]), even_w_in, i,
                row(jnp.tile(q_norm[i], 2)), row(jnp.tile(k_norm[i], 2)), tm_in)
            o_attn = _swa_attention(sinks[i], q, k, v, nsub)
            mats = _ssm_prepare(ssm_a_re[i], ssm_a_im[i], ssm_b_re[i], ssm_b_im[i],
                                ssm_c_re[i], ssm_c_im[i], ssm_log_dt[i], b)
            o_ssm = _ssm_mixer(u, mats, row(ssm_d[i]), ssm_w_glu, i, row(ssm_b_glu[i]), steps)
            mixes, w_out = [o_attn, o_ssm], even_w_out
        else:
            q, k, v = _odd_in_proj(x, row(odd_norm[i]), odd_w_in, i, tm_in)
            mixes, w_out = [_sb_attention(q, k, v, th, SB_PAIRS_PER_STEP)], odd_w_out
        x = _post(x, mixes, w_out, i, row(ffn_norm[layer]), ffn_w_gate, ffn_w_up, ffn_w_down,
                  layer, tm)
    return x
```

```python
import functools
import math

import jax
import jax.numpy as jnp
from jax import lax
from jax.experimental import pallas as pl
from jax.experimental.pallas import tpu as pltpu

F32 = jnp.float32
BF16 = jnp.bfloat16

D_MODEL = 1024
HEAD_DIM = 64
EPS = 1e-6
LANES = 128

SWA_Q_HEADS = 8
SWA_KV_HEADS = 2
SWA_BLOCK = 128
SWA_WIDTH = SWA_Q_HEADS * HEAD_DIM
KV_WIDTH = SWA_KV_HEADS * HEAD_DIM

SSM_WIDTH = D_MODEL // 2
SSM_GROUP_CH = 16
SSM_GROUPS = SSM_WIDTH // SSM_GROUP_CH
SSM_STATE = 64
SSM_STATES = SSM_GROUPS * SSM_STATE
SSM_CHUNKS = SSM_WIDTH // LANES
SSM_CHUNK_STATES = SSM_STATES // SSM_CHUNKS
EVEN_IN = SWA_WIDTH + 2 * KV_WIDTH + SSM_WIDTH

SB_HEADS = D_MODEL // HEAD_DIM
SB_WIDTH = SB_HEADS * HEAD_DIM
LOG2E = math.log2(math.e)
SB_DEAD_LOG2 = 160.0
SB_EXP2_CLAMP = 126.0
SB_PAIRS_PER_STEP = 4
SB_STREAM_GROUP = 2

VMEM_LIMIT_BYTES = 56 * 1024 * 1024
ROW_SPLIT = 2


def _params(*sem):
    return pltpu.CompilerParams(dimension_semantics=sem, vmem_limit_bytes=VMEM_LIMIT_BYTES)


def _const_spec(shape):
    nd = len(shape)
    return pl.BlockSpec(shape, lambda *_: (0,) * nd, pipeline_mode=pl.Buffered(1))


def _layer_spec(stacked, layer):
    rest = stacked.shape[1:]
    return pl.BlockSpec((None, *rest), lambda *_: (layer,) + (0,) * len(rest),
                        pipeline_mode=pl.Buffered(1))


def _row_subblocks(tm):
    step = tm // ROW_SPLIT
    return [slice(i * step, (i + 1) * step) for i in range(ROW_SPLIT)]


def _rmsnorm(x, g):
    ms = jnp.mean(x * x, axis=-1, keepdims=True)
    return x * lax.rsqrt(ms + EPS) * g


def _pair_rmsnorm(x, g2):
    lo = lax.broadcasted_iota(jnp.int32, x.shape, 1) < HEAD_DIM
    sq = x * x
    s_lo = jnp.sum(jnp.where(lo, sq, 0.0), axis=-1, keepdims=True)
    s_hi = jnp.sum(jnp.where(lo, 0.0, sq), axis=-1, keepdims=True)
    ms = jnp.where(lo, s_lo, s_hi) * (1.0 / HEAD_DIM)
    return x * lax.rsqrt(ms + EPS) * g2


def _even_in_kernel(x_ref, g_ref, w_ref, qg_ref, kg_ref, q_ref, k_ref, v_ref, u_ref):
    subs = _row_subblocks(x_ref.shape[1])
    hns = [_rmsnorm(x_ref[0, rows, :], g_ref[...]).astype(BF16) for rows in subs]
    projs = [jnp.dot(hn, w_ref[...], preferred_element_type=F32) for hn in hns]
    scale = HEAD_DIM ** -0.5 * LOG2E
    for proj, rows in zip(projs, subs):
        for p in range(SWA_WIDTH // LANES):
            qp = _pair_rmsnorm(proj[:, p * LANES:(p + 1) * LANES], qg_ref[...])
            q_ref[0, rows, p * LANES:(p + 1) * LANES] = (qp * scale).astype(BF16)
        k = _pair_rmsnorm(proj[:, SWA_WIDTH:SWA_WIDTH + KV_WIDTH], kg_ref[...])
        k_ref[0, rows, :] = k.astype(BF16)
        v_ref[0, rows, :] = proj[:, SWA_WIDTH + KV_WIDTH:SWA_WIDTH + 2 * KV_WIDTH].astype(BF16)
        u_ref[0, rows, :] = proj[:, SWA_WIDTH + 2 * KV_WIDTH:]


def _even_in_proj(x, g, w, layer, qg, kg, tm):
    b, l, d = x.shape
    return pl.pallas_call(
        _even_in_kernel,
        grid=(b, l // tm),
        in_specs=[
            pl.BlockSpec((1, tm, d), lambda i, j: (i, j, 0)),
            _const_spec((1, d)),
            _layer_spec(w, layer),
            _const_spec((1, LANES)),
            _const_spec((1, LANES)),
        ],
        out_specs=[
            pl.BlockSpec((1, tm, SWA_WIDTH), lambda i, j: (i, j, 0)),
            pl.BlockSpec((1, tm, KV_WIDTH), lambda i, j: (i, j, 0)),
            pl.BlockSpec((1, tm, KV_WIDTH), lambda i, j: (i, j, 0)),
            pl.BlockSpec((1, tm, SSM_WIDTH), lambda i, j: (i, j, 0)),
        ],
        out_shape=[
            jax.ShapeDtypeStruct((b, l, SWA_WIDTH), BF16),
            jax.ShapeDtypeStruct((b, l, KV_WIDTH), BF16),
            jax.ShapeDtypeStruct((b, l, KV_WIDTH), BF16),
            jax.ShapeDtypeStruct((b, l, SSM_WIDTH), F32),
        ],
        compiler_params=_params("parallel", "parallel"),
        name="even_in_proj",
    )(x, g, w, qg, kg)


def _swa_kernel(sink_ref, q_ref, kc_ref, kp_ref, vc_ref, vp_ref, o_ref, *, nsub):
    n = pl.program_id(1)
    blk = SWA_BLOCK
    kall = jnp.concatenate([kp_ref[0], kc_ref[0]], axis=0).astype(F32)
    vall = jnp.concatenate([vp_ref[0], vc_ref[0]], axis=0).astype(F32)
    lo = lax.broadcasted_iota(jnp.int32, kall.shape, 1) < HEAD_DIM

    def halves(t):
        g0_lo = jnp.where(lo, t, 0.0)
        g1_hi = jnp.where(lo, 0.0, t)
        g0_hi = pltpu.roll(g0_lo, HEAD_DIM, 1)
        g1_lo = pltpu.roll(g1_hi, HEAD_DIM, 1)
        return [[g0_lo.astype(BF16), g0_hi.astype(BF16)], [g1_lo.astype(BF16), g1_hi.astype(BF16)]]

    ks, vs = halves(kall), halves(vall)
    qq = lax.broadcasted_iota(jnp.int32, (blk, 2 * blk), 0)
    kk = lax.broadcasted_iota(jnp.int32, (blk, 2 * blk), 1)
    diff = qq + blk - kk
    band = (diff >= 0) & (diff < blk)
    first_key = jnp.where(n > 0, 0, blk)
    band_first = band & (kk >= first_key)

    streams = []
    for s in range(nsub):
        keys = slice(s * blk, (s + 2) * blk)
        for p in range(SWA_WIDTH // LANES):
            g = (2 * p) // (SWA_Q_HEADS // SWA_KV_HEADS)
            for e in range(2):
                streams.append(dict(
                    q=q_ref[0, s * blk:(s + 1) * blk, p * LANES:(p + 1) * LANES],
                    k=ks[g][e][keys], v=vs[g][e][keys],
                    sink=sink_ref[2 * p + e] * LOG2E, mask=band_first if s == 0 else band))
    group = len(streams) // nsub
    for g0 in range(0, len(streams), group):
        for st in streams[g0:g0 + group]:
            st["z"] = lax.dot_general(st["q"], st["k"], (((1,), (1,)), ((), ())),
                                      preferred_element_type=F32)
        for st in streams[g0:g0 + group]:
            z = jnp.where(st["mask"], st.pop("z"), -jnp.inf)
            m = jnp.maximum(jnp.max(z, axis=-1, keepdims=True), st["sink"])
            pe = jnp.exp2(z - m)
            st["den"] = jnp.sum(pe, axis=-1, keepdims=True) + jnp.exp2(st["sink"] - m)
            st["pe"] = pe.astype(BF16)
        for st in streams[g0:g0 + group]:
            st["o"] = jnp.dot(st.pop("pe"), st["v"], preferred_element_type=F32) / st["den"]
    for s in range(nsub):
        for p in range(SWA_WIDTH // LANES):
            i = (s * (SWA_WIDTH // LANES) + p) * 2
            o_ref[0, s * blk:(s + 1) * blk, p * LANES:(p + 1) * LANES] = (
                streams[i]["o"] + streams[i + 1]["o"]).astype(BF16)


def _swa_attention(sinks, q, k, v, nsub):
    b, l, _ = q.shape
    blk = SWA_BLOCK * nsub
    cur = lambda i, j: (i, j, 0)
    prev = lambda i, j: (i, jnp.maximum(j * nsub - 1, 0), 0)
    return pl.pallas_call(
        functools.partial(_swa_kernel, nsub=nsub),
        grid=(b, l // blk),
        in_specs=[
            pl.BlockSpec(memory_space=pltpu.SMEM),
            pl.BlockSpec((1, blk, SWA_WIDTH), cur),
            pl.BlockSpec((1, blk, KV_WIDTH), cur),
            pl.BlockSpec((1, SWA_BLOCK, KV_WIDTH), prev),
            pl.BlockSpec((1, blk, KV_WIDTH), cur),
            pl.BlockSpec((1, SWA_BLOCK, KV_WIDTH), prev),
        ],
        out_specs=pl.BlockSpec((1, blk, SWA_WIDTH), cur),
        out_shape=jax.ShapeDtypeStruct((b, l, SWA_WIDTH), BF16),
        compiler_params=_params("parallel", "parallel"),
        name="swa_attention",
    )(sinks, q, k, k, v, v)


def _ssm_disc_kernel(are_ref, aim_ref, ldt_ref, lre_ref, lim_ref, wre_ref, wim_ref):
    a_re, a_im = are_ref[...], aim_ref[...]
    dt = jnp.exp(ldt_ref[...])
    mag = jnp.exp(a_re * dt)
    lam_re = mag * jnp.cos(a_im * dt)
    lam_im = mag * jnp.sin(a_im * dt)
    den = a_re * a_re + a_im * a_im
    lre_ref[...] = lam_re
    lim_ref[...] = lam_im
    wre_ref[...] = ((lam_re - 1.0) * a_re + lam_im * a_im) / den
    wim_ref[...] = (lam_im * a_re - (lam_re - 1.0) * a_im) / den


def _ssm_bbar_kernel(wre_ref, wim_ref, bre_ref, bim_ref, ore_ref, oim_ref):
    w_re, w_im, b_re, b_im = wre_ref[...], wim_ref[...], bre_ref[...], bim_ref[...]
    ore_ref[...] = w_re * b_re - w_im * b_im
    oim_ref[...] = w_re * b_im + w_im * b_re


def _block_diag(t):
    ch, gl, r, c = t.shape
    eye = jnp.eye(gl, dtype=t.dtype)
    return (t[:, :, :, None, :] * eye[None, :, None, :, None]).reshape(ch, gl * r, gl * c)


def _ssm_prepare(a_re, a_im, b_re, b_im, c_re, c_im, log_dt, batch):
    g, n, p = SSM_GROUPS, SSM_STATE, SSM_GROUP_CH
    gn = jax.ShapeDtypeStruct((g, n), F32)
    ldt = jnp.broadcast_to(log_dt[:, None], (g, n))
    lam_re, lam_im, w_re, w_im = pl.pallas_call(
        _ssm_disc_kernel, out_shape=[gn, gn, gn, gn], name="ssm_discretise")(a_re, a_im, ldt)
    gnp = jax.ShapeDtypeStruct((g, n * p), F32)
    bb_re, bb_im = pl.pallas_call(_ssm_bbar_kernel, out_shape=[gnp, gnp], name="ssm_bbar")(
        jnp.repeat(w_re, p, axis=1), jnp.repeat(w_im, p, axis=1),
        b_re.reshape(g, n * p), b_im.reshape(g, n * p))
    gl = g // SSM_CHUNKS
    to_in = lambda t: _block_diag(
        t.reshape(SSM_CHUNKS, gl, n, p).transpose(0, 1, 3, 2)).astype(BF16)
    to_out = lambda t: _block_diag(
        t.reshape(SSM_CHUNKS, gl, p, n).transpose(0, 1, 3, 2)).astype(BF16)
    bcast = lambda t: jnp.broadcast_to(t.reshape(1, SSM_STATES), (batch, SSM_STATES))
    return (to_in(bb_re), to_in(bb_im), bcast(lam_re), bcast(lam_im), to_out(c_re), to_out(c_im))


def _ssm_kernel(u_ref, bre_ref, bim_ref, lre_ref, lim_ref, cre_ref, cim_ref, d_ref, wg_ref, bg_ref,
                o_ref, bm_ref, utm_ref, h0_ref, h1_ref, h2_ref, h3_ref, sre_ref, sim_ref,
                *, batch, steps, pitch):
    @pl.when(pl.program_id(0) == 0)
    def _():
        sre_ref[...] = jnp.zeros_like(sre_ref)
        sim_ref[...] = jnp.zeros_like(sim_ref)

    lane_tiles = u_ref.shape[-1] // LANES
    for b in range(batch):
        for i in range(lane_tiles):
            bm_ref[i, b * pitch:b * pitch + steps, :] = u_ref[b, :, i * LANES:(i + 1) * LANES]
    for t in range(steps):
        for i in range(lane_tiles):
            utm_ref[t * batch:(t + 1) * batch, i * LANES:(i + 1) * LANES] = (
                bm_ref[i, pl.ds(t, batch, stride=pitch), :])
    u = utm_ref[...]
    ub = u.astype(BF16)
    tiles = SSM_CHUNK_STATES // LANES
    ys = [None] * SSM_CHUNKS
    h_refs = (h0_ref, h1_ref, h2_ref, h3_ref)

    def input_matmul(c, part):
        w_ref = (bre_ref, bim_ref)[part]
        bu = jnp.dot(ub[:, c * LANES:(c + 1) * LANES], w_ref[c], preferred_element_type=F32)
        for i in range(tiles):
            h_refs[c][part, i] = bu[:, i * LANES:(i + 1) * LANES]

    def output_matmul(c):
        hs = [jnp.concatenate([h_refs[c][part, i] for i in range(tiles)], axis=1).astype(BF16)
              for part in range(2)]
        ys[c] = (jnp.dot(hs[0], cre_ref[c], preferred_element_type=F32)
                 - jnp.dot(hs[1], cim_ref[c], preferred_element_type=F32))

    def scan(chunks, fillers):
        slabs = [(c, i) for c in chunks for i in range(tiles)]
        lanes = [slice((c * tiles + i) * LANES, (c * tiles + i + 1) * LANES) for c, i in slabs]
        h_re = [sre_ref[:, ln] for ln in lanes]
        h_im = [sim_ref[:, ln] for ln in lanes]
        l_re = [lre_ref[:, ln] for ln in lanes]
        l_im = [lim_ref[:, ln] for ln in lanes]
        fillers = list(fillers)
        every = max(1, steps // max(1, len(fillers)))
        for t in range(steps):
            rows = slice(t * batch, (t + 1) * batch)
            for i, (c, tile) in enumerate(slabs):
                n_re = l_re[i] * h_re[i] - l_im[i] * h_im[i] + h_refs[c][0, tile, rows, :]
                n_im = l_re[i] * h_im[i] + l_im[i] * h_re[i] + h_refs[c][1, tile, rows, :]
                h_refs[c][0, tile, rows, :] = n_re
                h_refs[c][1, tile, rows, :] = n_im
                h_re[i], h_im[i] = n_re, n_im
            if fillers and (t + 1) % every == 0:
                fillers.pop(0)()
        for f in fillers:
            f()
        for i, ln in enumerate(lanes):
            sre_ref[:, ln] = h_re[i]
            sim_ref[:, ln] = h_im[i]

    first, second = (0, 1), (2, 3)
    for c in first:
        input_matmul(c, 0)
        input_matmul(c, 1)
    scan(first, [functools.partial(input_matmul, c, part) for c in second for part in range(2)])
    scan(second, [functools.partial(output_matmul, c) for c in first])
    for c in second:
        output_matmul(c)
    y = jnp.concatenate(ys, axis=-1) + d_ref[...] * u
    y = jax.nn.gelu(y)
    gate = jnp.dot(y.astype(BF16), wg_ref[...], preferred_element_type=F32) + bg_ref[...]
    out = y * jax.nn.sigmoid(gate)
    for t in range(steps):
        for i in range(lane_tiles):
            bm_ref[i, pl.ds(t, batch, stride=pitch), :] = (
                out[t * batch:(t + 1) * batch, i * LANES:(i + 1) * LANES])
    for b in range(batch):
        o_ref[b] = jnp.concatenate(
            [bm_ref[i, b * pitch:b * pitch + steps, :] for i in range(lane_tiles)],
            axis=1).astype(BF16)


def _ssm_mixer(u, mats, d_skip, w_glu, layer, b_glu, steps):
    batch, l, width = u.shape
    assert steps % 8 == 0
    pitch = steps + 4
    b_in_re, b_in_im, lam_re, lam_im, c_out_re, c_out_im = mats
    block = pl.BlockSpec((batch, steps, width), lambda i: (0, i, 0))
    return pl.pallas_call(
        functools.partial(_ssm_kernel, batch=batch, steps=steps, pitch=pitch),
        grid=(l // steps,),
        in_specs=[
            block,
            _const_spec(b_in_re.shape), _const_spec(b_in_im.shape),
            _const_spec(lam_re.shape), _const_spec(lam_im.shape),
            _const_spec(c_out_re.shape), _const_spec(c_out_im.shape),
            _const_spec((1, width)), _layer_spec(w_glu, layer), _const_spec((1, width)),
        ],
        out_specs=block,
        out_shape=jax.ShapeDtypeStruct((batch, l, width), BF16),
        scratch_shapes=[
            pltpu.VMEM((width // LANES, batch * pitch, LANES), F32),
            pltpu.VMEM((batch * steps, width), F32),
            *[pltpu.VMEM((2, SSM_CHUNK_STATES // LANES, batch * steps, LANES), F32)
              for _ in range(SSM_CHUNKS)],
            pltpu.VMEM((batch, SSM_STATES), F32), pltpu.VMEM((batch, SSM_STATES), F32),
        ],
        compiler_params=_params("arbitrary"),
        name="ssm_mixer",
    )(u, b_in_re, b_in_im, lam_re, lam_im, c_out_re, c_out_im, d_skip, w_glu, b_glu)


def _odd_in_kernel(x_ref, g_ref, w_ref, q_ref, k_ref, v_ref):
    subs = _row_subblocks(x_ref.shape[1])
    hns = [_rmsnorm(x_ref[0, rows, :], g_ref[...]).astype(BF16) for rows in subs]
    projs = [jnp.dot(hn, w_ref[...], preferred_element_type=F32) for hn in hns]
    for proj, rows in zip(projs, subs):
        q_ref[0, rows, :] = (proj[:, :SB_WIDTH] * (HEAD_DIM ** -0.5 * LOG2E)).astype(BF16)
        k_ref[0, rows, :] = proj[:, SB_WIDTH:2 * SB_WIDTH].astype(BF16)
        v_ref[0, rows, :] = proj[:, 2 * SB_WIDTH:].astype(BF16)


def _odd_in_proj(x, g, w, layer, tm):
    b, l, d = x.shape
    row = pl.BlockSpec((1, tm, SB_WIDTH), lambda i, j: (i, j, 0))
    out = jax.ShapeDtypeStruct((b, l, SB_WIDTH), BF16)
    return pl.pallas_call(
        _odd_in_kernel,
        grid=(b, l // tm),
        in_specs=[pl.BlockSpec((1, tm, d), lambda i, j: (i, j, 0)),
                  _const_spec((1, d)), _layer_spec(w, layer)],
        out_specs=[row, row, row],
        out_shape=[out, out, out],
        compiler_params=_params("parallel", "parallel"),
        name="odd_in_proj",
    )(x, g, w)


def _sb_streams(streams, tri):
    m = streams[0][0].shape[0]
    half = m // 2
    strict = (lax.broadcasted_iota(jnp.int32, (half, half), 1)
              < lax.broadcasted_iota(jnp.int32, (half, half), 0))

    def lower_triangle(fn, *blocks):
        quad = lambda r, c: fn(*[blk[r * half:(r + 1) * half, c * half:(c + 1) * half]
                                 for blk in blocks])
        top_left = jnp.where(strict, quad(0, 0), 0.0)
        bottom_right = jnp.where(strict, quad(1, 1), 0.0)
        top = jnp.concatenate([top_left, jnp.zeros_like(top_left)], axis=1)
        return jnp.concatenate([top, jnp.concatenate([quad(1, 0), bottom_right], axis=1)], axis=0)

    def softplus2(z):
        return jnp.maximum(z, jnp.log(1.0 + jnp.exp2(jnp.minimum(z, SB_EXP2_CLAMP))) * LOG2E)

    def scores(s):
        s["z"] = lax.dot_general(s["q"], s["k"], (((1,), (1,)), ((), ())),
                                 preferred_element_type=F32)

    def softplus(s):
        sp = lower_triangle(softplus2, s["z"]) if s["diag"] else softplus2(s["z"])
        s["sp"] = sp.astype(BF16)

    def suffix_sums(s):
        s["w"] = jnp.dot(s.pop("sp"), tri, preferred_element_type=F32)

    def weights(s):
        weight = lambda z, w: jnp.exp2(z - w)
        z = s.pop("z")
        p = lower_triangle(weight, z, s["w"]) if s["diag"] else weight(z, s["w"])
        s["p"] = p.astype(BF16)

    def values(s):
        s["pv"] = jnp.dot(s.pop("p"), s["v"], preferred_element_type=F32)

    stages = (scores, softplus, suffix_sums, weights, values)
    state = [dict(q=q, k=kh, v=vb, diag=diag) for q, kh, vb, diag in streams]
    groups = [state[i:i + SB_STREAM_GROUP] for i in range(0, len(state), SB_STREAM_GROUP)]
    for tick in range(len(groups) + len(stages) - 1):
        for g, members in enumerate(groups):
            if 0 <= tick - g < len(stages):
                for s in members:
                    stages[tick - g](s)
    return [(s["pv"], jnp.broadcast_to(s["w"][:, 0:1], (s["pv"].shape[0], LANES))) for s in state]


def _sb_kernel(q_ref, k_ref, v_ref, tri_ref, o_ref, c_ref, acc_ref, *, th, npair):
    qi = pl.program_id(2)
    lo_k = lax.broadcasted_iota(jnp.int32, (th, LANES), 1) < HEAD_DIM
    lo_q = lax.broadcasted_iota(jnp.int32, (th, LANES), 1) < HEAD_DIM
    tri = tri_ref[...]

    def load_kv(pair, j):
        rows = pl.ds(pl.multiple_of(j * th, th), th)
        lanes = slice(pair * LANES, (pair + 1) * LANES)
        kb = k_ref[0, rows, lanes]
        return (jnp.where(lo_k, kb, 0), jnp.where(lo_k, 0, kb)), v_ref[0, rows, lanes]

    def load_q(pair, half):
        return q_ref[0, half * th:(half + 1) * th, pair * LANES:(pair + 1) * LANES]

    j_diag = (2 * qi, 2 * qi + 1)
    has_left = (qi > 0).astype(F32)

    streams = []
    for pair in range(npair):
        kv_left = load_kv(pair, jnp.maximum(2 * qi - 1, 0))
        kv_diag = (load_kv(pair, j_diag[0]), load_kv(pair, j_diag[1]))
        for half in range(2):
            q = load_q(pair, half)
            (kd, vd) = kv_diag[half]
            (kl, vl) = kv_diag[0] if half == 1 else kv_left
            for h in range(2):
                streams += [(q, kd[h], vd, True), (q, kl[h], vl, False)]
    outs = iter(_sb_streams(streams, tri))
    for pair in range(npair):
        for half in range(2):
            for h in range(2):
                (pv_d, c_d), (pv_l, c_l) = next(outs), next(outs)
                scale = jnp.exp2(-c_d)
                if half == 0:
                    scale, c_l = scale * has_left, c_l * has_left
                acc_ref[pair, half, h] = pv_d + pv_l * scale
                c_ref[pair, half, h] = c_d + c_l

    def visit_rest(pair, half):
        q = load_q(pair, half)
        n_rest = jnp.maximum(j_diag[half] - 1, 0)

        def cond(carry):
            i, c_min = carry
            return jnp.logical_and(i < n_rest, c_min < SB_DEAD_LOG2)

        def body(carry):
            i, _ = carry
            ks, vb = load_kv(pair, n_rest - 1 - i)
            outs = _sb_streams([(q, ks[0], vb, False), (q, ks[1], vb, False)], tri)
            for h, (pv, c_blk) in enumerate(outs):
                c = c_ref[pair, half, h]
                acc_ref[pair, half, h] += pv * jnp.exp2(-c)
                c_ref[pair, half, h] = c + c_blk
            return i + 1, jnp.min(c_ref[pair, half])

        lax.while_loop(cond, body, (jnp.int32(0), jnp.min(c_ref[pair, half])))

    @pl.when(jnp.min(c_ref[...]) < SB_DEAD_LOG2)
    def _():
        for pair in range(npair):
            for half in range(2):
                visit_rest(pair, half)

    for pair in range(npair):
        for half in range(2):
            o_ref[0, half * th:(half + 1) * th, pair * LANES:(pair + 1) * LANES] = jnp.where(
                lo_q, acc_ref[pair, half, 0], acc_ref[pair, half, 1]).astype(BF16)


def _sb_attention(q, k, v, th, npair):
    b, l, width = q.shape
    tri = (jnp.arange(th)[:, None] >= jnp.arange(th)[None, :]).astype(BF16)
    tq = 2 * th
    lanes = npair * LANES
    state = pltpu.VMEM((npair, 2, 2, th, LANES), F32)
    return pl.pallas_call(
        functools.partial(_sb_kernel, th=th, npair=npair),
        grid=(b, width // lanes, l // tq),
        in_specs=[
            pl.BlockSpec((1, tq, lanes), lambda i, h, j: (i, j, h)),
            pl.BlockSpec((1, l, lanes), lambda i, h, j: (i, 0, h)),
            pl.BlockSpec((1, l, lanes), lambda i, h, j: (i, 0, h)),
            _const_spec((th, th)),
        ],
        out_specs=pl.BlockSpec((1, tq, lanes), lambda i, h, j: (i, j, h)),
        out_shape=jax.ShapeDtypeStruct((b, l, width), BF16),
        scratch_shapes=[state, state],
        compiler_params=_params("parallel", "parallel", "arbitrary"),
        name="sb_attention",
    )(q, k, v, tri)


def _post_kernel(*refs, n_mix):
    x_ref = refs[0]
    mix_refs = refs[1:1 + n_mix]
    wo_ref, g_ref, wg_ref, wu_ref, wd_ref, o_ref = refs[1 + n_mix:]
    subs = _row_subblocks(x_ref.shape[1])
    mixed = [jnp.concatenate([m_ref[0, rows, :] for m_ref in mix_refs], axis=-1) for rows in subs]
    xs = [x_ref[0, rows, :] + jnp.dot(m, wo_ref[...], preferred_element_type=F32)
          for m, rows in zip(mixed, subs)]
    hns = [_rmsnorm(x, g_ref[...]).astype(BF16) for x in xs]
    gates = [jnp.dot(hn, wg_ref[...], preferred_element_type=F32) for hn in hns]
    ups = [jnp.dot(hn, wu_ref[...], preferred_element_type=F32) for hn in hns]
    acts = [(gate * jax.nn.sigmoid(gate) * up).astype(BF16) for gate, up in zip(gates, ups)]
    for x, act, rows in zip(xs, acts, subs):
        o_ref[0, rows, :] = x + jnp.dot(act, wd_ref[...], preferred_element_type=F32)


def _post(x, mixes, w_out, mixer_layer, g, w_gate, w_up, w_down, layer, tm):
    b, l, d = x.shape
    xspec = pl.BlockSpec((1, tm, d), lambda i, j: (i, j, 0))
    mix_specs = [pl.BlockSpec((1, tm, m.shape[-1]), lambda i, j: (i, j, 0)) for m in mixes]
    return pl.pallas_call(
        functools.partial(_post_kernel, n_mix=len(mixes)),
        grid=(b, l // tm),
        in_specs=[xspec, *mix_specs, _layer_spec(w_out, mixer_layer),
                  _const_spec((1, d)), _layer_spec(w_gate, layer), _layer_spec(w_up, layer),
                  _layer_spec(w_down, layer)],
        out_specs=xspec,
        out_shape=jax.ShapeDtypeStruct(x.shape, F32),
        compiler_params=_params("parallel", "parallel"),
        name="out_proj_ffn",
    )(x, *mixes, w_out, g, w_gate, w_up, w_down)


def _tiles(l):
    tm = min(512, l)
    tm_in = min(1024, l)
    th = min(256, l // 2)
    steps = min(128, l)
    nsub = min(8, l // SWA_BLOCK)
    return tm, tm_in, th, steps, nsub


def kernel(x, even_norm, even_w_in, q_norm, k_norm, sinks, ssm_a_re, ssm_a_im, ssm_b_re, ssm_b_im,
           ssm_c_re, ssm_c_im, ssm_d, ssm_log_dt, ssm_w_glu, ssm_b_glu, even_w_out, odd_norm,
           odd_w_in, odd_w_out, ffn_norm, ffn_w_gate, ffn_w_up, ffn_w_down):
    b, l, d = x.shape
    assert d == D_MODEL and l % SWA_BLOCK == 0
    tm, tm_in, th, steps, nsub = _tiles(l)
    depth = ffn_norm.shape[0]
    row = lambda t: t.reshape(1, -1)
    (even_w_in, ssm_w_glu, even_w_out, odd_w_in, odd_w_out, ffn_w_gate, ffn_w_up, ffn_w_down) = (
        t.astype(BF16) for t in (even_w_in, ssm_w_glu, even_w_out, odd_w_in, odd_w_out,
                                 ffn_w_gate, ffn_w_up, ffn_w_down))
    for layer in range(depth):
        i = layer // 2
        if layer % 2 == 0:
            q, k, v, u = _even_in_proj(
                x, row(even_norm[i]), even_w_in, i,
                row(jnp.tile(q_norm[i], 2)), row(jnp.tile(k_norm[i], 2)), tm_in)
            o_attn = _swa_attention(sinks[i], q, k, v, nsub)
            mats = _ssm_prepare(ssm_a_re[i], ssm_a_im[i], ssm_b_re[i], ssm_b_im[i],
                                ssm_c_re[i], ssm_c_im[i], ssm_log_dt[i], b)
            o_ssm = _ssm_mixer(u, mats, row(ssm_d[i]), ssm_w_glu, i, row(ssm_b_glu[i]), steps)
            mixes, w_out = [o_attn, o_ssm], even_w_out
        else:
            q, k, v = _odd_in_proj(x, row(odd_norm[i]), odd_w_in, i, tm_in)
            mixes, w_out = [_sb_attention(q, k, v, th, SB_PAIRS_PER_STEP)], odd_w_out
        x = _post(x, mixes, w_out, i, row(ffn_norm[layer]), ffn_w_gate, ffn_w_up, ffn_w_down,
                  layer, tm)
    return x
```
